```python
import jax, jax.numpy as jnp
from jax import lax
import numpy as np

D_MODEL = 1024
BATCH = 8
SEQ = 8192
DEPTH = 2
DEC_BATCH = 1
DEC_SEQ = 16384
PAST_LEN = 128

D_CONV = D_MODEL
CONV_WIDTH = 3
D_FOURIER = D_MODEL // 2
N_FOURIER_GROUPS = 4
FOURIER_GROUP = D_FOURIER // N_FOURIER_GROUPS
D_IN = 3 * D_CONV + D_FOURIER + 2 * D_MODEL
N_EXPERTS = 32
TOP_K = 4
D_FF = D_MODEL
SWIGLU_ALPHA = 1.702
SWIGLU_LIMIT = 7.0
EXPERT_BLOCK = 256
N_MOD = 6
EPS = 1e-5

kernel_name = 'hybrid_conv_fourier_moe_encoder'


def rmsnorm(x, g):
    xf = x.astype(jnp.float32)
    xf = xf * lax.rsqrt(jnp.mean(xf * xf, axis=-1, keepdims=True) + EPS)
    return (xf * g.astype(jnp.float32)).astype(x.dtype)


def short_conv(u, w, b):
    s = u.shape[1]
    up = jnp.pad(u, ((0, 0), (1, 1), (0, 0)))
    return w[0] * up[:, :s] + w[1] * up[:, 1:s + 1] + w[2] * up[:, 2:s + 2] + b


def fourier_mix(u):
    bsz, s, _ = u.shape
    ug = u.astype(jnp.float32).reshape(bsz, s, N_FOURIER_GROUPS, FOURIER_GROUP).transpose(0, 2, 1, 3)
    f = jnp.real(jnp.fft.fft2(ug, norm='ortho'))
    return f.transpose(0, 2, 1, 3).reshape(bsz, s, D_FOURIER).astype(u.dtype)


def token_mixer(h, w_in, conv_w, conv_b, w_conv_out, w_fourier_out, w_o):
    cuts = (D_CONV, 2 * D_CONV, 3 * D_CONV, 3 * D_CONV + D_FOURIER, 3 * D_CONV + D_FOURIER + D_MODEL)
    proj = jnp.einsum('bsd,de->bse', h, w_in)
    gate_b, gate_c, v, u_f, z_a, z_f = jnp.split(proj, cuts, axis=-1)
    y_a = jnp.einsum('bsc,cd->bsd', gate_b * short_conv(gate_c * v, conv_w, conv_b), w_conv_out)
    y_f = jnp.einsum('bsf,fd->bsd', fourier_mix(u_f), w_fourier_out)
    merged = jax.nn.sigmoid(z_a) * y_a + jax.nn.sigmoid(z_f) * y_f
    return jnp.einsum('bsd,de->bse', merged, w_o)


def expert_ffn(xb, w_gu, b_gu, w_dn, b_dn):
    gu = xb @ w_gu + b_gu
    glu = jnp.minimum(gu[:, :D_FF], SWIGLU_LIMIT)
    lin = jnp.clip(gu[:, D_FF:], -SWIGLU_LIMIT, SWIGLU_LIMIT)
    act = glu * jax.nn.sigmoid(SWIGLU_ALPHA * glu) * (lin + 1.0)
    return act @ w_dn + b_dn


def moe(h, router_w, router_b, w_gate_up, b_gate_up, w_down, b_down):
    n_tok = h.shape[0]
    n_asg = n_tok * TOP_K
    n_blocks = -(-n_asg // EXPERT_BLOCK) + N_EXPERTS
    n_rows = n_blocks * EXPERT_BLOCK
    logits = h.astype(jnp.float32) @ router_w.astype(jnp.float32) + router_b.astype(jnp.float32)
    top_val, top_idx = lax.top_k(logits, TOP_K)
    top_w = jax.nn.softmax(top_val, axis=-1).astype(h.dtype)
    flat_e = top_idx.reshape(-1)
    flat_tok = jnp.repeat(jnp.arange(n_tok, dtype=jnp.int32), TOP_K)
    flat_w = top_w.reshape(-1)
    order = jnp.argsort(flat_e)
    e_sorted = flat_e[order]
    counts = jnp.bincount(flat_e, length=N_EXPERTS)
    padded = (counts + EXPERT_BLOCK - 1) // EXPERT_BLOCK * EXPERT_BLOCK
    start = jnp.cumsum(counts) - counts
    pad_end = jnp.cumsum(padded)
    pad_start = pad_end - padded
    rank = jnp.arange(n_asg, dtype=jnp.int32) - start[e_sorted]
    dest = pad_start[e_sorted] + rank
    tok_buf = jnp.full((n_rows,), n_tok, dtype=jnp.int32).at[dest].set(flat_tok[order])
    w_buf = jnp.zeros((n_rows,), h.dtype).at[dest].set(flat_w[order])
    block_start = jnp.arange(n_blocks, dtype=jnp.int32) * EXPERT_BLOCK
    block_expert = jnp.minimum(jnp.searchsorted(pad_end, block_start, side='right'), N_EXPERTS - 1)
    h_pad = jnp.concatenate([h, jnp.zeros((1, h.shape[1]), h.dtype)], axis=0)

    def run_block(args):
        tok, wts, e = args
        out = expert_ffn(h_pad[tok], w_gate_up[e], b_gate_up[e], w_down[e], b_down[e])
        return out * wts[:, None]

    out = lax.map(run_block, (tok_buf.reshape(n_blocks, EXPERT_BLOCK),
                              w_buf.reshape(n_blocks, EXPERT_BLOCK), block_expert))
    y = jnp.zeros_like(h_pad).at[tok_buf].add(out.reshape(n_rows, -1))
    return y[:n_tok]


def encoder_layer(x, c, mod_w, mod_b, norm1_g, norm2_g, w_in, conv_w, conv_b, w_conv_out,
                  w_fourier_out, w_o, router_w, router_b, w_gate_up, b_gate_up, w_down, b_down):
    mod = jax.nn.silu(c) @ mod_w + mod_b
    shift1, scale1, gate1, shift2, scale2, gate2 = jnp.split(mod[:, None, :], N_MOD, axis=-1)
    h = rmsnorm(x, norm1_g) * (1.0 + scale1) + shift1
    x = x + gate1 * token_mixer(h, w_in, conv_w, conv_b, w_conv_out, w_fourier_out, w_o)
    h = rmsnorm(x, norm2_g) * (1.0 + scale2) + shift2
    bsz, s, d = x.shape
    y = moe(h.reshape(bsz * s, d), router_w, router_b, w_gate_up, b_gate_up, w_down, b_down)
    return x + gate2 * y.reshape(bsz, s, d)


def trunk(x, c, mod_w, mod_b, norm1_g, norm2_g, w_in, conv_w, conv_b, w_conv_out,
          w_fourier_out, w_o, router_w, router_b, w_gate_up, b_gate_up, w_down, b_down, final_g):
    for l in range(DEPTH):
        x = encoder_layer(x, c, mod_w[l], mod_b[l], norm1_g[l], norm2_g[l], w_in[l], conv_w[l],
                          conv_b[l], w_conv_out[l], w_fourier_out[l], w_o[l], router_w[l],
                          router_b[l], w_gate_up[l], b_gate_up[l], w_down[l], b_down[l])
    return rmsnorm(x, final_g)


def setup_inputs(seed: int = 0) -> dict:
    key = jax.random.key(seed)
    ks = jax.random.split(key, 21)

    def nrm(k, shape, scale):
        return jax.random.normal(k, shape, jnp.float32) * scale

    return {
        'x_prompt': nrm(ks[0], (BATCH, SEQ, D_MODEL), 1.0),
        'x_sample': nrm(ks[1], (DEC_BATCH, DEC_SEQ, D_MODEL), 1.0),
        'c_prompt': nrm(ks[2], (BATCH, D_MODEL), 1.0),
        'c_sample': nrm(ks[3], (DEC_BATCH, D_MODEL), 1.0),
        'mod_w': nrm(ks[4], (DEPTH, D_MODEL, N_MOD * D_MODEL), 0.5 * D_MODEL ** -0.5),
        'mod_b': nrm(ks[5], (DEPTH, N_MOD * D_MODEL), 0.02),
        'norm1_g': 1.0 + nrm(ks[6], (DEPTH, D_MODEL), 0.02),
        'norm2_g': 1.0 + nrm(ks[7], (DEPTH, D_MODEL), 0.02),
        'w_in': nrm(ks[8], (DEPTH, D_MODEL, D_IN), D_MODEL ** -0.5),
        'conv_w': nrm(ks[9], (DEPTH, CONV_WIDTH, D_CONV), CONV_WIDTH ** -0.5),
        'conv_b': nrm(ks[10], (DEPTH, D_CONV), 0.02),
        'w_conv_out': nrm(ks[11], (DEPTH, D_CONV, D_MODEL), D_CONV ** -0.5),
        'w_fourier_out': nrm(ks[12], (DEPTH, D_FOURIER, D_MODEL), D_FOURIER ** -0.5),
        'w_o': nrm(ks[13], (DEPTH, D_MODEL, D_MODEL), D_MODEL ** -0.5),
        'router_w': nrm(ks[14], (DEPTH, D_MODEL, N_EXPERTS), D_MODEL ** -0.5),
        'router_b': nrm(ks[15], (DEPTH, N_EXPERTS), 0.01),
        'w_gate_up': nrm(ks[16], (DEPTH, N_EXPERTS, D_MODEL, 2 * D_FF), D_MODEL ** -0.5),
        'b_gate_up': nrm(ks[17], (DEPTH, N_EXPERTS, 2 * D_FF), 0.02),
        'w_down': nrm(ks[18], (DEPTH, N_EXPERTS, D_FF, D_MODEL), D_FF ** -0.5),
        'b_down': nrm(ks[19], (DEPTH, N_EXPERTS, D_MODEL), 0.02),
        'final_g': 1.0 + nrm(ks[20], (D_MODEL,), 0.02),
    }


def reference(x_prompt, x_sample, c_prompt, c_sample, mod_w, mod_b, norm1_g, norm2_g, w_in,
              conv_w, conv_b, w_conv_out, w_fourier_out, w_o, router_w, router_b, w_gate_up,
              b_gate_up, w_down, b_down, final_g):
    y_prompt = trunk(x_prompt, c_prompt, mod_w, mod_b, norm1_g, norm2_g, w_in, conv_w, conv_b,
                     w_conv_out, w_fourier_out, w_o, router_w, router_b, w_gate_up, b_gate_up,
                     w_down, b_down, final_g)
    y_sample = trunk(x_sample, c_sample, mod_w, mod_b, norm1_g, norm2_g, w_in, conv_w, conv_b,
                     w_conv_out, w_fourier_out, w_o, router_w, router_b, w_gate_up, b_gate_up,
                     w_down, b_down, final_g)
    return (y_prompt, y_sample)
```

```python
import functools
import math

import numpy as np
import jax
import jax.numpy as jnp
from jax import lax
from jax.experimental import pallas as pl
from jax.experimental.pallas import tpu as pltpu

F32 = jnp.float32
BF16 = jnp.bfloat16
I32 = jnp.int32

D = 1024
DF = 512
NG = 4
DG = DF // NG
E = 32
TOPK = 4
DFF = 1024
N_MOD = 6
ALPHA = 1.702
LIMIT = 7.0
EPS = 1e-5
LANES = 128
FFT_N1 = 128
FFT_K1_TILE = 8
VMEM_LIMIT = 56 * 1024 * 1024
NEG = -1e30


def _params(sem, vmem=VMEM_LIMIT):
    return pltpu.CompilerParams(dimension_semantics=sem, vmem_limit_bytes=vmem)


class _Geom:
    def __init__(self, bp, sp, bs, ss):
        self.bp, self.sp, self.bs, self.ss = bp, sp, bs, ss
        self.tp = bp * sp
        self.tt = self.tp + bs * ss
        self.nseq = bp + bs

    def seq_of(self, t0):
        return jnp.where(t0 < self.tp, t0 // self.sp, self.bp + (t0 - self.tp) // self.ss)

    def offset_len(self, t0):
        in_p = t0 < self.tp
        off = jnp.where(in_p, t0 % self.sp, (t0 - self.tp) % self.ss)
        slen = jnp.where(in_p, self.sp, self.ss)
        return off, slen


def _mod_body(c_ref, w_ref, b_ref, o_ref):
    c = c_ref[...]
    s = c * jax.nn.sigmoid(c)
    o_ref[...] = jnp.dot(s, w_ref[...], preferred_element_type=F32,
                         precision=lax.Precision.HIGHEST) + b_ref[...]


def _modulation(c_pad, mod_w, mod_b):
    rows = c_pad.shape[0]
    return pl.pallas_call(
        _mod_body,
        grid=(N_MOD,),
        in_specs=[pl.BlockSpec((rows, D), lambda j: (0, 0)),
                  pl.BlockSpec((D, D), lambda j: (0, j)),
                  pl.BlockSpec((1, D), lambda j: (0, j))],
        out_specs=pl.BlockSpec((rows, D), lambda j: (0, j)),
        out_shape=jax.ShapeDtypeStruct((rows, N_MOD * D), F32),
        compiler_params=_params(("parallel",)),
        name="modulation",
    )(c_pad, mod_w, mod_b.reshape(1, N_MOD * D))


def _inproj_body(x_ref, mod_ref, g_ref, w_ref, cat_ref,
                 gb_ref, p_ref, sa_ref, sf_ref, a_ref, b_ref):
    x = x_ref[...]
    ms = jnp.mean(x * x, axis=-1, keepdims=True)
    xn = x * lax.rsqrt(ms + EPS) * g_ref[...]
    shift = mod_ref[:, 0:D]
    scale = mod_ref[:, D:2 * D]
    h = (xn * (1.0 + scale) + shift).astype(BF16)

    def proj(lo, hi):
        return jnp.dot(h, w_ref[:, lo:hi], preferred_element_type=F32)

    gb_ref[...] = proj(0, D).astype(BF16)
    p_ref[...] = (proj(D, 2 * D) * proj(2 * D, 3 * D)).astype(BF16)
    uf = proj(3 * D, 3 * D + DF).astype(BF16)
    for g in range(NG):
        ab = jnp.dot(uf[:, g * DG:(g + 1) * DG], cat_ref[...], preferred_element_type=F32)
        a_ref[:, g * DG:(g + 1) * DG] = ab[:, :DG].astype(BF16)
        b_ref[:, g * DG:(g + 1) * DG] = ab[:, DG:].astype(BF16)
    sa_ref[...] = jax.nn.sigmoid(proj(3 * D + DF, 4 * D + DF)).astype(BF16)
    sf_ref[...] = jax.nn.sigmoid(proj(4 * D + DF, 5 * D + DF)).astype(BF16)


def _inproj(geom, tm, x, mod3, g1, w_in_bf, cat):
    tt = geom.tt
    row = lambda i: (i, 0)
    full = lambda i: (0, 0)
    bf = lambda n: jax.ShapeDtypeStruct((tt, n), BF16)
    return pl.pallas_call(
        _inproj_body,
        grid=(tt // tm,),
        in_specs=[pl.BlockSpec((tm, D), row),
                  pl.BlockSpec((None, 1, N_MOD * D), lambda i: (geom.seq_of(i * tm), 0, 0)),
                  pl.BlockSpec((1, D), full),
                  pl.BlockSpec(w_in_bf.shape, full),
                  pl.BlockSpec(cat.shape, full)],
        out_specs=[pl.BlockSpec((tm, D), row), pl.BlockSpec((tm, D), row),
                   pl.BlockSpec((tm, D), row), pl.BlockSpec((tm, D), row),
                   pl.BlockSpec((tm, DF), row), pl.BlockSpec((tm, DF), row)],
        out_shape=[bf(D), bf(D), bf(D), bf(D), bf(DF), bf(DF)],
        compiler_params=_params(("parallel",)),
        name="inproj",
    )(x, mod3, g1, w_in_bf, cat)


@functools.lru_cache(maxsize=None)
def _fft_consts(s):
    n1 = FFT_N1
    n2 = s // n1
    a = np.arange(n1)
    ang = 2.0 * np.pi * (np.outer(a, a) % n1) / n1
    c1, s1 = np.cos(ang), np.sin(ang)
    w1 = np.block([[c1, s1], [-s1, c1]])
    ang = 2.0 * np.pi * (np.outer(np.arange(n1), np.arange(n2)) % s) / s
    tc = np.repeat(np.cos(ang), LANES, axis=1)
    ts = np.repeat(np.sin(ang), LANES, axis=1)
    m = np.arange(n2)
    ang = 2.0 * np.pi * (np.outer(m, m) % n2) / n2
    c2, s2 = np.cos(ang), np.sin(ang)
    kt = FFT_K1_TILE
    wexp = np.zeros((n2, kt, 2, kt, n2))
    for j in range(kt):
        wexp[:, j, 0, j, :] = c2
        wexp[:, j, 1, j, :] = s2
    wexp = wexp.reshape(n2 * kt, 2 * kt * n2)
    return (np.asarray(w1, np.float32), np.asarray(tc, np.float32), np.asarray(ts, np.float32),
            np.asarray(wexp, np.float32))


@functools.lru_cache(maxsize=None)
def _chan_dft():
    a = np.arange(DG)
    ang = 2.0 * np.pi * (np.outer(a, a) % DG) / DG
    return np.asarray(np.concatenate([np.cos(ang), -np.sin(ang)], axis=1), np.float32)


def _fft1_body(a_ref, b_ref, w_ref, tc_ref, ts_ref, yr_ref, yi_ref, *, tn2):
    ab = jnp.concatenate([a_ref[...], b_ref[...]], axis=0)
    y = jnp.dot(w_ref[...], ab, preferred_element_type=F32)
    n1 = FFT_N1
    for j in range(tn2):
        c = tc_ref[:, j * LANES:(j + 1) * LANES]
        s = ts_ref[:, j * LANES:(j + 1) * LANES]
        for g in range(NG):
            lo = (j * NG + g) * LANES
            yr = y[:n1, lo:lo + LANES]
            yi = y[n1:, lo:lo + LANES]
            yr_ref[:, lo:lo + LANES] = (yr * c + yi * s).astype(BF16)
            yi_ref[:, lo:lo + LANES] = (yi * c - yr * s).astype(BF16)


def _fft2_body(yr_ref, yi_ref, w_ref, f_ref, *, n2, scale):
    rhs = jnp.concatenate([yr_ref[...], yi_ref[...]], axis=0)
    z = jnp.dot(w_ref[...], rhs, preferred_element_type=F32) * scale
    f_ref[...] = z.reshape(n2, FFT_K1_TILE, DF)


def _fourier(a, b, nseq, s):
    n1 = FFT_N1
    n2 = s // n1
    tn2 = 8
    kt = FFT_K1_TILE
    w1, tc, ts, wexp = _fft_consts(s)
    a3 = a.reshape(nseq, n1, n2 * DF)
    b3 = b.reshape(nseq, n1, n2 * DF)
    blk = lambda q, j: (q, 0, j)
    yr, yi = pl.pallas_call(
        functools.partial(_fft1_body, tn2=tn2),
        grid=(nseq, n2 // tn2),
        in_specs=[pl.BlockSpec((None, n1, tn2 * DF), blk),
                  pl.BlockSpec((None, n1, tn2 * DF), blk),
                  pl.BlockSpec((2 * n1, 2 * n1), lambda q, j: (0, 0)),
                  pl.BlockSpec((n1, tn2 * LANES), lambda q, j: (0, j)),
                  pl.BlockSpec((n1, tn2 * LANES), lambda q, j: (0, j))],
        out_specs=[pl.BlockSpec((None, n1, tn2 * DF), blk),
                   pl.BlockSpec((None, n1, tn2 * DF), blk)],
        out_shape=[jax.ShapeDtypeStruct((nseq, n1, n2 * DF), BF16)] * 2,
        compiler_params=_params(("parallel", "parallel")),
        name="fft_stage1",
    )(a3, b3, jnp.asarray(w1, BF16), jnp.asarray(tc), jnp.asarray(ts))
    yr = yr.reshape(nseq, n1 * n2, DF)
    yi = yi.reshape(nseq, n1 * n2, DF)
    f = pl.pallas_call(
        functools.partial(_fft2_body, n2=n2, scale=1.0 / math.sqrt(s * DG)),
        grid=(nseq, n1 // kt),
        in_specs=[pl.BlockSpec((None, kt * n2, DF), lambda q, j: (q, j, 0)),
                  pl.BlockSpec((None, kt * n2, DF), lambda q, j: (q, j, 0)),
                  pl.BlockSpec(wexp.shape, lambda q, j: (0, 0))],
        out_specs=pl.BlockSpec((None, n2, kt, DF), lambda q, j: (q, 0, j, 0)),
        out_shape=jax.ShapeDtypeStruct((nseq, n2, n1, DF), F32),
        compiler_params=_params(("parallel", "parallel")),
        name="fft_stage2",
    )(yr, yi, jnp.asarray(wexp, BF16))
    return f.reshape(nseq * s, DF)


def _mixer_body(x_ref, gb_ref, p_ref, pprev_ref, pnext_ref, sa_ref, sf_ref, f_ref, mod_ref,
                cw_ref, cb_ref, wco_ref, wfo_ref, wo_ref, g2_ref, rwh_ref, rwl_ref, rb_ref,
                ltri_ref, x1_ref, h2_ref, meta_ref, wts_ref, cnt_ref, carry, *, tm, geom, halo):
    i = pl.program_id(0)

    @pl.when(i == 0)
    def _():
        carry[...] = jnp.zeros_like(carry)

    off, slen = geom.offset_len(i * tm)
    keep_prev = jnp.where(off == 0, 0.0, 1.0)
    keep_next = jnp.where(off + tm == slen, 0.0, 1.0)
    p = p_ref[...].astype(F32)
    prev_row = pprev_ref[halo - 1:halo, :].astype(F32) * keep_prev
    next_row = pnext_ref[0:1, :].astype(F32) * keep_next
    row = lax.broadcasted_iota(I32, (tm, 1), 0)
    pm = jnp.where(row == 0, prev_row, pltpu.roll(p, 1, 0))
    pp = jnp.where(row == tm - 1, next_row, pltpu.roll(p, tm - 1, 0))
    conv = cw_ref[0:1, :] * pm + cw_ref[1:2, :] * p + cw_ref[2:3, :] * pp + cb_ref[...]
    ya_in = (gb_ref[...].astype(F32) * conv).astype(BF16)
    ya = jnp.dot(ya_in, wco_ref[...], preferred_element_type=F32)
    yf = jnp.dot(f_ref[...].astype(BF16), wfo_ref[...], preferred_element_type=F32)
    merged = sa_ref[...].astype(F32) * ya + sf_ref[...].astype(F32) * yf
    mix = jnp.dot(merged.astype(BF16), wo_ref[...], preferred_element_type=F32)
    x1 = x_ref[...] + mod_ref[:, 2 * D:3 * D] * mix
    x1_ref[...] = x1

    ms = jnp.mean(x1 * x1, axis=-1, keepdims=True)
    xn = x1 * lax.rsqrt(ms + EPS) * g2_ref[...]
    h2 = xn * (1.0 + mod_ref[:, 4 * D:5 * D]) + mod_ref[:, 3 * D:4 * D]
    h2_ref[...] = h2

    hh = h2.astype(BF16)
    hl = (h2 - hh.astype(F32)).astype(BF16)
    logits = (jnp.dot(hh, rwh_ref[...], preferred_element_type=F32)
              + jnp.dot(hh, rwl_ref[...], preferred_element_type=F32)
              + jnp.dot(hl, rwh_ref[...], preferred_element_type=F32)) + rb_ref[...]

    lane = lax.broadcasted_iota(I32, (tm, LANES), 1)
    l = logits
    vals, idxs = [], []
    for _ in range(TOPK):
        m = jnp.max(l, axis=-1, keepdims=True)
        ix = jnp.min(jnp.where(l == m, lane, LANES), axis=-1, keepdims=True)
        vals.append(m)
        idxs.append(ix)
        l = jnp.where(lane == ix, -jnp.inf, l)
    exps = [jnp.exp(v - vals[0]) for v in vals]
    denom = exps[0] + exps[1] + exps[2] + exps[3]
    ws = [e / denom for e in exps]

    onehot = jnp.zeros((tm, LANES), F32)
    for ix in idxs:
        onehot = onehot + jnp.where(lane == ix, 1.0, 0.0)
    before = jnp.dot(ltri_ref[...], onehot.astype(BF16), preferred_element_type=F32) + carry[0:1, :]
    ranks = [jnp.sum(jnp.where(lane == ix, before, 0.0), axis=-1, keepdims=True).astype(I32)
             for ix in idxs]
    carry[...] = carry[...] + jnp.sum(onehot, axis=0, keepdims=True)
    cnt_ref[...] = carry[...]

    meta = jnp.zeros((tm, LANES), I32)
    wts = jnp.zeros((tm, LANES), F32)
    for k in range(TOPK):
        meta = jnp.where(lane == k, idxs[k], meta)
        meta = jnp.where(lane == TOPK + k, ranks[k], meta)
        wts = jnp.where(lane == k, ws[k], wts)
    meta_ref[...] = meta
    wts_ref[...] = wts


def _mixer(geom, tm, x, gb, p, sa, sf, f, mod3, conv_w, conv_b, wco, wfo, wo, g2, rwh, rwl, rb, ltri):
    tt = geom.tt
    halo = 16
    hb = tm // halo
    nhalo = tt // halo
    row = lambda i: (i, 0)
    full = lambda i: (0, 0)
    body = functools.partial(_mixer_body, tm=tm, geom=geom, halo=halo)
    return pl.pallas_call(
        body,
        grid=(tt // tm,),
        in_specs=[pl.BlockSpec((tm, D), row),
                  pl.BlockSpec((tm, D), row),
                  pl.BlockSpec((tm, D), row),
                  pl.BlockSpec((halo, D), lambda i: (jnp.maximum(i * hb - 1, 0), 0)),
                  pl.BlockSpec((halo, D), lambda i: (jnp.minimum((i + 1) * hb, nhalo - 1), 0)),
                  pl.BlockSpec((tm, D), row),
                  pl.BlockSpec((tm, D), row),
                  pl.BlockSpec((tm, DF), row),
                  pl.BlockSpec((None, 1, N_MOD * D), lambda i: (geom.seq_of(i * tm), 0, 0)),
                  pl.BlockSpec((3, D), full),
                  pl.BlockSpec((1, D), full),
                  pl.BlockSpec((D, D), full),
                  pl.BlockSpec((DF, D), full),
                  pl.BlockSpec((D, D), full),
                  pl.BlockSpec((1, D), full),
                  pl.BlockSpec((D, LANES), full),
                  pl.BlockSpec((D, LANES), full),
                  pl.BlockSpec((1, LANES), full),
                  pl.BlockSpec((tm, tm), full)],
        out_specs=[pl.BlockSpec((tm, D), row),
                   pl.BlockSpec((tm, D), row),
                   pl.BlockSpec((tm, LANES), row),
                   pl.BlockSpec((tm, LANES), row),
                   pl.BlockSpec((8, LANES), full)],
        out_shape=[jax.ShapeDtypeStruct((tt, D), F32),
                   jax.ShapeDtypeStruct((tt, D), F32),
                   jax.ShapeDtypeStruct((tt, LANES), I32),
                   jax.ShapeDtypeStruct((tt, LANES), F32),
                   jax.ShapeDtypeStruct((8, LANES), F32)],
        scratch_shapes=[pltpu.VMEM((8, LANES), F32)],
        compiler_params=_params(("arbitrary",)),
        name="mixer_router",
    )(x, gb, p, p, p, sa, sf, f, mod3, conv_w, conv_b, wco, wfo, wo, g2, rwh, rwl, rb, ltri)


def _dispatch_body(dest_ref, h_ref, zeros_hbm, xs_hbm, sem, *, tq):
    del zeros_hbm

    def copy(t, k):
        d = dest_ref[TOPK * t + k]
        return pltpu.make_async_copy(h_ref.at[pl.ds(t, 1), :], xs_hbm.at[pl.ds(d, 1), :], sem)

    def issue(t, c):
        for k in range(TOPK):
            copy(t, k).start()
        return c

    def drain(t, c):
        for k in range(TOPK):
            copy(t, k).wait()
        return c

    lax.fori_loop(0, tq, issue, 0)
    lax.fori_loop(0, tq, drain, 0)


def _dispatch(tq, dest, h2, n_rows):
    tt = h2.shape[0]
    zeros = jnp.zeros((n_rows, D), F32)
    return pl.pallas_call(
        functools.partial(_dispatch_body, tq=tq),
        grid=(tt // tq,),
        in_specs=[pl.BlockSpec((tq * TOPK,), lambda i: (i,), memory_space=pltpu.SMEM),
                  pl.BlockSpec((tq, D), lambda i: (i, 0)),
                  pl.BlockSpec(memory_space=pl.ANY)],
        out_specs=pl.BlockSpec(memory_space=pl.ANY),
        out_shape=jax.ShapeDtypeStruct((n_rows, D), F32),
        scratch_shapes=[pltpu.SemaphoreType.DMA],
        input_output_aliases={2: 0},
        compiler_params=_params(("arbitrary",)),
        name="moe_dispatch",
    )(dest, h2, zeros)


def _ffn_body(be_ref, nu_ref, xs_ref, wgu_ref, bgu_ref, wdn_ref, bdn_ref, os_ref):
    del be_ref

    @pl.when(pl.program_id(0) < nu_ref[0])
    def _():
        x = xs_ref[...].astype(BF16)
        gu = jnp.dot(x, wgu_ref[...], preferred_element_type=F32) + bgu_ref[...]
        glu = jnp.minimum(gu[:, :DFF], LIMIT)
        lin = jnp.clip(gu[:, DFF:], -LIMIT, LIMIT)
        act = glu * jax.nn.sigmoid(ALPHA * glu) * (lin + 1.0)
        os_ref[...] = jnp.dot(act.astype(BF16), wdn_ref[...], preferred_element_type=F32) + bdn_ref[...]


def _expert_ffn(bm, block_expert, n_used, xs, wgu, bgu, wdn, bdn):
    n_rows = xs.shape[0]
    n_blocks = n_rows // bm
    rows = lambda i, be, nu: (jnp.minimum(i, nu[0] - 1), 0)
    wsel = lambda i, be, nu: (be[i], 0, 0)
    grid_spec = pltpu.PrefetchScalarGridSpec(
        num_scalar_prefetch=2,
        grid=(n_blocks,),
        in_specs=[pl.BlockSpec((bm, D), rows),
                  pl.BlockSpec((None, D, 2 * DFF), wsel),
                  pl.BlockSpec((None, 1, 2 * DFF), wsel),
                  pl.BlockSpec((None, DFF, D), wsel),
                  pl.BlockSpec((None, 1, D), wsel)],
        out_specs=pl.BlockSpec((bm, D), rows),
    )
    return pl.pallas_call(
        _ffn_body,
        grid_spec=grid_spec,
        out_shape=jax.ShapeDtypeStruct((n_rows, D), F32),
        compiler_params=_params(("arbitrary",)),
        name="expert_ffn",
    )(block_expert, n_used, xs, wgu, bgu, wdn, bdn)


def _combine_body(dest_ref, os_hbm, w_ref, x1_ref, mod_ref, g_ref, o_ref, stage, sem, *, tq, final):
    def copy(t, k):
        d = dest_ref[TOPK * t + k]
        return pltpu.make_async_copy(os_hbm.at[pl.ds(d, 1), :], stage.at[k, pl.ds(t, 1), :], sem)

    def issue(t, c):
        for k in range(TOPK):
            copy(t, k).start()
        return c

    def drain(t, c):
        for k in range(TOPK):
            copy(t, k).wait()
        return c

    lax.fori_loop(0, tq, issue, 0)
    lax.fori_loop(0, tq, drain, 0)
    y = w_ref[:, 0:1] * stage[0]
    for k in range(1, TOPK):
        y = y + w_ref[:, k:k + 1] * stage[k]
    x2 = x1_ref[...] + mod_ref[:, 5 * D:6 * D] * y
    if final:
        ms = jnp.mean(x2 * x2, axis=-1, keepdims=True)
        x2 = x2 * lax.rsqrt(ms + EPS) * g_ref[...]
    o_ref[...] = x2


def _combine(geom, tq, dest, os_rows, wts, x1, mod3, g_final, final):
    tt = geom.tt
    return pl.pallas_call(
        functools.partial(_combine_body, tq=tq, final=final),
        grid=(tt // tq,),
        in_specs=[pl.BlockSpec((tq * TOPK,), lambda i: (i,), memory_space=pltpu.SMEM),
                  pl.BlockSpec(memory_space=pl.ANY),
                  pl.BlockSpec((tq, LANES), lambda i: (i, 0)),
                  pl.BlockSpec((tq, D), lambda i: (i, 0)),
                  pl.BlockSpec((None, 1, N_MOD * D), lambda i: (geom.seq_of(i * tq), 0, 0)),
                  pl.BlockSpec((1, D), lambda i: (0, 0))],
        out_specs=pl.BlockSpec((tq, D), lambda i: (i, 0)),
        out_shape=jax.ShapeDtypeStruct((tt, D), F32),
        scratch_shapes=[pltpu.VMEM((TOPK, tq, D), F32), pltpu.SemaphoreType.DMA],
        compiler_params=_params(("arbitrary",)),
        name="moe_combine",
    )(dest, os_rows, wts, x1, mod3, g_final)


def _route_plan(meta, counts_f, bm, n_blocks):
    idx = meta[:, 0:TOPK]
    rank = meta[:, TOPK:2 * TOPK]
    counts = counts_f[0, :E].astype(I32)
    padded = (counts + bm - 1) // bm * bm
    pad_end = jnp.cumsum(padded)
    pad_start = pad_end - padded
    dest = (pad_start[idx] + rank).reshape(-1)
    n_used = (pad_end[E - 1] // bm).astype(I32)
    blk = jnp.arange(n_blocks, dtype=I32)
    be = jnp.minimum(jnp.searchsorted(pad_end, blk * bm, side='right'), E - 1).astype(I32)
    be = jnp.where(blk < n_used, be, be[n_used - 1])
    return dest, be, n_used.reshape(1)


def _trunk(geom, x, c_all, mod_w, mod_b, norm1_g, norm2_g, w_in, conv_w, conv_b, w_conv_out,
           w_fourier_out, w_o, router_w, router_b, w_gate_up, b_gate_up, w_down, b_down, final_g,
           tm=512, tq=256, bm=512):
    depth = mod_w.shape[0]
    tt = geom.tt
    n_blocks = -(-tt * TOPK // bm) + E
    n_rows = n_blocks * bm
    cat = jnp.asarray(_chan_dft(), BF16)
    ltri = jnp.asarray(np.tril(np.ones((tm, tm), np.float32), -1), BF16)
    c_pad = jnp.zeros((-(-geom.nseq // 8) * 8, D), F32).at[:geom.nseq].set(c_all)
    for l in range(depth):
        mod = _modulation(c_pad, mod_w[l], mod_b[l])
        mod3 = mod.reshape(mod.shape[0], 1, N_MOD * D)
        gb, p, sa, sf, a, b = _inproj(geom, tm, x, mod3, norm1_g[l].reshape(1, D),
                                      w_in[l].astype(BF16), cat)
        f_parts = []
        if geom.bp:
            f_parts.append(_fourier(a[:geom.tp], b[:geom.tp], geom.bp, geom.sp))
        if geom.bs:
            f_parts.append(_fourier(a[geom.tp:], b[geom.tp:], geom.bs, geom.ss))
        f = jnp.concatenate(f_parts, axis=0) if len(f_parts) > 1 else f_parts[0]
        rw = jnp.zeros((D, LANES), F32).at[:, :E].set(router_w[l])
        rwh = rw.astype(BF16)
        rwl = (rw - rwh.astype(F32)).astype(BF16)
        rb = jnp.full((1, LANES), NEG, F32).at[0, :E].set(router_b[l])
        x1, h2, meta, wts, counts = _mixer(
            geom, tm, x, gb, p, sa, sf, f, mod3, conv_w[l], conv_b[l].reshape(1, D),
            w_conv_out[l].astype(BF16), w_fourier_out[l].astype(BF16), w_o[l].astype(BF16),
            norm2_g[l].reshape(1, D), rwh, rwl, rb, ltri)
        dest, block_expert, n_used = _route_plan(meta, counts, bm, n_blocks)
        xs = _dispatch(tq, dest, h2, n_rows)
        os_rows = _expert_ffn(bm, block_expert, n_used, xs,
                              w_gate_up[l].astype(BF16), b_gate_up[l].reshape(E, 1, 2 * DFF),
                              w_down[l].astype(BF16), b_down[l].reshape(E, 1, D))
        x = _combine(geom, tq, dest, os_rows, wts, x1, mod3, final_g.reshape(1, D),
                     final=(l == depth - 1))
    return x


def kernel(x_prompt, x_sample, c_prompt, c_sample, mod_w, mod_b, norm1_g, norm2_g, w_in, conv_w,
           conv_b, w_conv_out, w_fourier_out, w_o, router_w, router_b, w_gate_up, b_gate_up,
           w_down, b_down, final_g):
    bp, sp, _ = x_prompt.shape
    bs, ss, _ = x_sample.shape
    geom = _Geom(bp, sp, bs, ss)
    x = jnp.concatenate([x_prompt.reshape(bp * sp, D), x_sample.reshape(bs * ss, D)], axis=0)
    c_all = jnp.concatenate([c_prompt, c_sample], axis=0)
    y = _trunk(geom, x, c_all, mod_w, mod_b, norm1_g, norm2_g, w_in, conv_w, conv_b, w_conv_out,
               w_fourier_out, w_o, router_w, router_b, w_gate_up, b_gate_up, w_down, b_down, final_g)
    return (y[:geom.tp].reshape(bp, sp, D), y[geom.tp:].reshape(bs, ss, D))
```

```python
import functools
import math

import numpy as np
import jax
import jax.numpy as jnp
from jax import lax
from jax.experimental import pallas as pl
from jax.experimental.pallas import tpu as pltpu

F32 = jnp.float32
BF16 = jnp.bfloat16
I32 = jnp.int32

D = 1024
DF = 512
NG = 4
DG = DF // NG
E = 32
TOPK = 4
DFF = 1024
N_MOD = 6
ALPHA = 1.702
LIMIT = 7.0
EPS = 1e-5
LANES = 128
SUBLANES = 8
ROW_TILE = D // LANES
FFT_N1 = 128
FFT_K1_TILE = SUBLANES
VMEM_LIMIT = 56 * 1024 * 1024
NEG = -1e30
DMA_GROUP = 8

assert ROW_TILE == SUBLANES


def _params(sem, vmem=VMEM_LIMIT):
    return pltpu.CompilerParams(dimension_semantics=sem, vmem_limit_bytes=vmem)


class _Geom:
    def __init__(self, bp, sp, bs, ss):
        assert bp > 0 and bs > 0
        self.bp, self.sp, self.bs, self.ss = bp, sp, bs, ss
        self.tp = bp * sp
        self.ts = bs * ss
        self.tt = self.tp + self.ts
        self.nseq = bp + bs

    def seq_of(self, t0):
        return jnp.where(t0 < self.tp, t0 // self.sp, self.bp + (t0 - self.tp) // self.ss)

    def offset_len(self, t0):
        in_p = t0 < self.tp
        off = jnp.where(in_p, t0 % self.sp, (t0 - self.tp) % self.ss)
        slen = jnp.where(in_p, self.sp, self.ss)
        return off, slen

    def split_specs(self, tm, width):
        npt = self.tp // tm
        return [pl.BlockSpec((tm, width), lambda i: (jnp.minimum(i, npt - 1), 0)),
                pl.BlockSpec((tm, width), lambda i: (jnp.maximum(i - npt, 0), 0))]


def _to_row_tiles(ref, val, rows):
    for s in range(ROW_TILE):
        ref[pl.ds(s, rows, stride=ROW_TILE), :] = val[:, s * LANES:(s + 1) * LANES]


def _from_row_tiles(ref, start, rows):
    return jnp.concatenate(
        [ref[pl.ds(start + s, rows, stride=ROW_TILE), :] for s in range(ROW_TILE)], axis=1)


def _mod_body(c_ref, w_ref, b_ref, o_ref):
    c = c_ref[...]
    s = c * jax.nn.sigmoid(c)
    o_ref[...] = jnp.dot(s, w_ref[...], preferred_element_type=F32,
                         precision=lax.Precision.HIGHEST) + b_ref[...]


def _modulation(c_pad, mod_w, mod_b):
    rows = c_pad.shape[0]
    return pl.pallas_call(
        _mod_body,
        grid=(N_MOD,),
        in_specs=[pl.BlockSpec((rows, D), lambda j: (0, 0)),
                  pl.BlockSpec((D, D), lambda j: (0, j)),
                  pl.BlockSpec((1, D), lambda j: (0, j))],
        out_specs=pl.BlockSpec((rows, D), lambda j: (0, j)),
        out_shape=jax.ShapeDtypeStruct((rows, N_MOD * D), F32),
        compiler_params=_params(("parallel",)),
        name="modulation",
    )(c_pad, mod_w, mod_b.reshape(1, N_MOD * D))


def _inproj_body(*refs, n_x, npt):
    x_refs = refs[:n_x]
    (mod_ref, g_ref, w_ref, cat_ref,
     gb_ref, p_ref, sa_ref, sf_ref, ap_ref, bp_ref, as_ref, bs_ref) = refs[n_x:]
    is_p = pl.program_id(0) < npt
    x = x_refs[0][...] if n_x == 1 else jnp.where(is_p, x_refs[0][...], x_refs[1][...])
    ms = jnp.mean(x * x, axis=-1, keepdims=True)
    xn = x * lax.rsqrt(ms + EPS) * g_ref[...]
    shift = mod_ref[:, 0:D]
    scale = mod_ref[:, D:2 * D]
    h = (xn * (1.0 + scale) + shift).astype(BF16)

    def proj(lo, hi):
        return jnp.dot(h, w_ref[:, lo:hi], preferred_element_type=F32)

    gb_ref[...] = proj(0, D).astype(BF16)
    p_ref[...] = (proj(D, 2 * D) * proj(2 * D, 3 * D)).astype(BF16)
    uf = proj(3 * D, 3 * D + DF).astype(BF16)
    abs_ = [jnp.dot(uf[:, g * DG:(g + 1) * DG], cat_ref[...], preferred_element_type=F32)
            for g in range(NG)]
    a = jnp.concatenate([ab[:, :DG] for ab in abs_], axis=1).astype(BF16)
    b = jnp.concatenate([ab[:, DG:] for ab in abs_], axis=1).astype(BF16)

    @pl.when(is_p)
    def _():
        ap_ref[...] = a
        bp_ref[...] = b

    @pl.when(jnp.logical_not(is_p))
    def _():
        as_ref[...] = a
        bs_ref[...] = b

    sa_ref[...] = jax.nn.sigmoid(proj(3 * D + DF, 4 * D + DF)).astype(BF16)
    sf_ref[...] = jax.nn.sigmoid(proj(4 * D + DF, 5 * D + DF)).astype(BF16)


def _inproj(geom, tm, xs, mod3, g1, w_in_bf, cat):
    tt = geom.tt
    npt = geom.tp // tm
    row = lambda i: (i, 0)
    full = lambda i: (0, 0)
    bf = lambda r, n: jax.ShapeDtypeStruct((r, n), BF16)
    x_specs = geom.split_specs(tm, D) if len(xs) == 2 else [pl.BlockSpec((tm, D), row)]
    ab_specs = geom.split_specs(tm, DF)
    return pl.pallas_call(
        functools.partial(_inproj_body, n_x=len(xs), npt=npt),
        grid=(tt // tm,),
        in_specs=x_specs + [
            pl.BlockSpec((None, 1, N_MOD * D), lambda i: (geom.seq_of(i * tm), 0, 0)),
            pl.BlockSpec((1, D), full),
            pl.BlockSpec(w_in_bf.shape, full),
            pl.BlockSpec(cat.shape, full)],
        out_specs=[pl.BlockSpec((tm, D), row), pl.BlockSpec((tm, D), row),
                   pl.BlockSpec((tm, D), row), pl.BlockSpec((tm, D), row),
                   ab_specs[0], ab_specs[0], ab_specs[1], ab_specs[1]],
        out_shape=[bf(tt, D), bf(tt, D), bf(tt, D), bf(tt, D),
                   bf(geom.tp, DF), bf(geom.tp, DF), bf(geom.ts, DF), bf(geom.ts, DF)],
        compiler_params=_params(("arbitrary",)),
        name="inproj",
    )(*xs, mod3, g1, w_in_bf, cat)


@functools.lru_cache(maxsize=None)
def _fft_consts(s):
    n1 = FFT_N1
    n2 = s // n1
    a = np.arange(n1)
    ang = 2.0 * np.pi * (np.outer(a, a) % n1) / n1
    c1, s1 = np.cos(ang), np.sin(ang)
    w1 = np.block([[c1, s1], [-s1, c1]])
    ang = 2.0 * np.pi * (np.outer(np.arange(n1), np.arange(n2)) % s) / s
    tc = np.repeat(np.cos(ang), LANES, axis=1)
    ts = np.repeat(np.sin(ang), LANES, axis=1)
    m = np.arange(n2)
    ang = 2.0 * np.pi * (np.outer(m, m) % n2) / n2
    c2, s2 = np.cos(ang), np.sin(ang)
    kt = FFT_K1_TILE
    wexp = np.zeros((n2, kt, 2, kt, n2))
    for j in range(kt):
        wexp[:, j, 0, j, :] = c2
        wexp[:, j, 1, j, :] = s2
    wexp = wexp.reshape(n2 * kt, 2 * kt * n2)
    return (np.asarray(w1, np.float32), np.asarray(tc, np.float32), np.asarray(ts, np.float32),
            np.asarray(wexp, np.float32))


@functools.lru_cache(maxsize=None)
def _chan_dft():
    a = np.arange(DG)
    ang = 2.0 * np.pi * (np.outer(a, a) % DG) / DG
    return np.asarray(np.concatenate([np.cos(ang), -np.sin(ang)], axis=1), np.float32)


def _fft1_body(a_ref, b_ref, w_ref, tc_ref, ts_ref, yr_ref, yi_ref, *, tn2):
    ab = jnp.concatenate([a_ref[...], b_ref[...]], axis=0)
    y = jnp.dot(w_ref[...], ab, preferred_element_type=F32)
    n1 = FFT_N1
    for j in range(tn2):
        c = tc_ref[:, j * LANES:(j + 1) * LANES]
        s = ts_ref[:, j * LANES:(j + 1) * LANES]
        for g in range(NG):
            lo = (j * NG + g) * LANES
            yr = y[:n1, lo:lo + LANES]
            yi = y[n1:, lo:lo + LANES]
            yr_ref[:, lo:lo + LANES] = (yr * c + yi * s).astype(BF16)
            yi_ref[:, lo:lo + LANES] = (yi * c - yr * s).astype(BF16)


def _fft2_body(yr_ref, yi_ref, w_ref, f_ref, *, n2, scale):
    rhs = jnp.concatenate([yr_ref[...], yi_ref[...]], axis=0)
    z = jnp.dot(w_ref[...], rhs, preferred_element_type=F32) * scale
    f_ref[...] = z.reshape(n2, FFT_K1_TILE, DF)


def _fourier(a, b, nseq, s):
    n1 = FFT_N1
    n2 = s // n1
    tn2 = 8
    kt = FFT_K1_TILE
    w1, tc, ts, wexp = _fft_consts(s)
    a3 = a.reshape(nseq, n1, n2 * DF)
    b3 = b.reshape(nseq, n1, n2 * DF)
    blk = lambda q, j: (q, 0, j)
    yr, yi = pl.pallas_call(
        functools.partial(_fft1_body, tn2=tn2),
        grid=(nseq, n2 // tn2),
        in_specs=[pl.BlockSpec((None, n1, tn2 * DF), blk),
                  pl.BlockSpec((None, n1, tn2 * DF), blk),
                  pl.BlockSpec((2 * n1, 2 * n1), lambda q, j: (0, 0)),
                  pl.BlockSpec((n1, tn2 * LANES), lambda q, j: (0, j)),
                  pl.BlockSpec((n1, tn2 * LANES), lambda q, j: (0, j))],
        out_specs=[pl.BlockSpec((None, n1, tn2 * DF), blk),
                   pl.BlockSpec((None, n1, tn2 * DF), blk)],
        out_shape=[jax.ShapeDtypeStruct((nseq, n1, n2 * DF), BF16)] * 2,
        compiler_params=_params(("parallel", "parallel")),
        name="fft_stage1",
    )(a3, b3, jnp.asarray(w1, BF16), jnp.asarray(tc), jnp.asarray(ts))
    yr = yr.reshape(nseq, n1 * n2, DF)
    yi = yi.reshape(nseq, n1 * n2, DF)
    f = pl.pallas_call(
        functools.partial(_fft2_body, n2=n2, scale=1.0 / math.sqrt(s * DG)),
        grid=(nseq, n1 // kt),
        in_specs=[pl.BlockSpec((None, kt * n2, DF), lambda q, j: (q, j, 0)),
                  pl.BlockSpec((None, kt * n2, DF), lambda q, j: (q, j, 0)),
                  pl.BlockSpec(wexp.shape, lambda q, j: (0, 0))],
        out_specs=pl.BlockSpec((None, n2, kt, DF), lambda q, j: (q, 0, j, 0)),
        out_shape=jax.ShapeDtypeStruct((nseq, n2, n1, DF), F32),
        compiler_params=_params(("parallel", "parallel")),
        name="fft_stage2",
    )(yr, yi, jnp.asarray(wexp, BF16))
    return f.reshape(nseq * s, DF)


def _mixer_body(*refs, n_x, npt, tm, geom, halo):
    x_refs = refs[:n_x]
    (gb_ref, p_ref, pprev_ref, pnext_ref, sa_ref, sf_ref, fp_ref, fs_ref, mod_ref,
     cw_ref, cb_ref, wco_ref, wfo_ref, wo_ref, g2_ref, rwh_ref, rwl_ref, rb_ref, ltri_ref,
     x1_ref, h2_ref, meta_ref, wts_ref, cnt_ref, carry) = refs[n_x:]
    i = pl.program_id(0)
    is_p = i < npt

    @pl.when(i == 0)
    def _():
        carry[...] = jnp.zeros_like(carry)

    x = x_refs[0][...] if n_x == 1 else jnp.where(is_p, x_refs[0][...], x_refs[1][...])
    f = jnp.where(is_p, fp_ref[...], fs_ref[...])

    off, slen = geom.offset_len(i * tm)
    keep_prev = jnp.where(off == 0, 0.0, 1.0)
    keep_next = jnp.where(off + tm == slen, 0.0, 1.0)
    p = p_ref[...].astype(F32)
    prev_row = pprev_ref[halo - 1:halo, :].astype(F32) * keep_prev
    next_row = pnext_ref[0:1, :].astype(F32) * keep_next
    row = lax.broadcasted_iota(I32, (tm, 1), 0)
    pm = jnp.where(row == 0, prev_row, pltpu.roll(p, 1, 0))
    pp = jnp.where(row == tm - 1, next_row, pltpu.roll(p, tm - 1, 0))
    conv = cw_ref[0:1, :] * pm + cw_ref[1:2, :] * p + cw_ref[2:3, :] * pp + cb_ref[...]
    ya_in = (gb_ref[...].astype(F32) * conv).astype(BF16)
    ya = jnp.dot(ya_in, wco_ref[...], preferred_element_type=F32)
    yf = jnp.dot(f.astype(BF16), wfo_ref[...], preferred_element_type=F32)
    merged = sa_ref[...].astype(F32) * ya + sf_ref[...].astype(F32) * yf
    mix = jnp.dot(merged.astype(BF16), wo_ref[...], preferred_element_type=F32)
    x1 = x + mod_ref[:, 2 * D:3 * D] * mix
    x1_ref[...] = x1

    ms = jnp.mean(x1 * x1, axis=-1, keepdims=True)
    xn = x1 * lax.rsqrt(ms + EPS) * g2_ref[...]
    h2 = xn * (1.0 + mod_ref[:, 4 * D:5 * D]) + mod_ref[:, 3 * D:4 * D]
    _to_row_tiles(h2_ref, h2, tm)

    hh = h2.astype(BF16)
    hl = (h2 - hh.astype(F32)).astype(BF16)
    logits = (jnp.dot(hh, rwh_ref[...], preferred_element_type=F32)
              + jnp.dot(hh, rwl_ref[...], preferred_element_type=F32)
              + jnp.dot(hl, rwh_ref[...], preferred_element_type=F32)) + rb_ref[...]

    lane = lax.broadcasted_iota(I32, (tm, LANES), 1)
    l = logits
    vals, idxs = [], []
    for _ in range(TOPK):
        m = jnp.max(l, axis=-1, keepdims=True)
        ix = jnp.min(jnp.where(l == m, lane, LANES), axis=-1, keepdims=True)
        vals.append(m)
        idxs.append(ix)
        l = jnp.where(lane == ix, -jnp.inf, l)
    exps = [jnp.exp(v - vals[0]) for v in vals]
    denom = exps[0] + exps[1] + exps[2] + exps[3]
    ws = [e / denom for e in exps]

    onehot = jnp.zeros((tm, LANES), F32)
    for ix in idxs:
        onehot = onehot + jnp.where(lane == ix, 1.0, 0.0)
    before = jnp.dot(ltri_ref[...], onehot.astype(BF16), preferred_element_type=F32) + carry[0:1, :]
    ranks = [jnp.sum(jnp.where(lane == ix, before, 0.0), axis=-1, keepdims=True).astype(I32)
             for ix in idxs]
    carry[...] = carry[...] + jnp.sum(onehot, axis=0, keepdims=True)
    cnt_ref[...] = carry[...]

    meta = jnp.zeros((tm, LANES), I32)
    wts = jnp.zeros((tm, LANES), F32)
    for k in range(TOPK):
        meta = jnp.where(lane == k, idxs[k], meta)
        meta = jnp.where(lane == TOPK + k, ranks[k], meta)
        wts = jnp.where(lane == k, ws[k], wts)
    meta_ref[...] = jnp.transpose(meta)[0:2 * TOPK, :]
    wts_ref[...] = wts


def _mixer(geom, tm, xs, gb, p, sa, sf, f_p, f_s, mod3, conv_w, conv_b, wco, wfo, wo, g2,
           rwh, rwl, rb, ltri):
    tt = geom.tt
    halo = 16
    hb = tm // halo
    nhalo = tt // halo
    row = lambda i: (i, 0)
    full = lambda i: (0, 0)
    x_specs = geom.split_specs(tm, D) if len(xs) == 2 else [pl.BlockSpec((tm, D), row)]
    body = functools.partial(_mixer_body, n_x=len(xs), npt=geom.tp // tm, tm=tm, geom=geom, halo=halo)
    return pl.pallas_call(
        body,
        grid=(tt // tm,),
        in_specs=x_specs + [
            pl.BlockSpec((tm, D), row),
            pl.BlockSpec((tm, D), row),
            pl.BlockSpec((halo, D), lambda i: (jnp.maximum(i * hb - 1, 0), 0)),
            pl.BlockSpec((halo, D), lambda i: (jnp.minimum((i + 1) * hb, nhalo - 1), 0)),
            pl.BlockSpec((tm, D), row),
            pl.BlockSpec((tm, D), row)] + geom.split_specs(tm, DF) + [
            pl.BlockSpec((None, 1, N_MOD * D), lambda i: (geom.seq_of(i * tm), 0, 0)),
            pl.BlockSpec((3, D), full),
            pl.BlockSpec((1, D), full),
            pl.BlockSpec((D, D), full),
            pl.BlockSpec((DF, D), full),
            pl.BlockSpec((D, D), full),
            pl.BlockSpec((1, D), full),
            pl.BlockSpec((D, LANES), full),
            pl.BlockSpec((D, LANES), full),
            pl.BlockSpec((1, LANES), full),
            pl.BlockSpec((tm, tm), full)],
        out_specs=[pl.BlockSpec((tm, D), row),
                   pl.BlockSpec((tm * ROW_TILE, LANES), row),
                   pl.BlockSpec((2 * TOPK, tm), lambda i: (0, i)),
                   pl.BlockSpec((tm, LANES), row),
                   pl.BlockSpec((8, LANES), full)],
        out_shape=[jax.ShapeDtypeStruct((tt, D), F32),
                   jax.ShapeDtypeStruct((tt * ROW_TILE, LANES), F32),
                   jax.ShapeDtypeStruct((2 * TOPK, tt), I32),
                   jax.ShapeDtypeStruct((tt, LANES), F32),
                   jax.ShapeDtypeStruct((8, LANES), F32)],
        scratch_shapes=[pltpu.VMEM((8, LANES), F32)],
        compiler_params=_params(("arbitrary",)),
        name="mixer_router",
    )(*xs, gb, p, p, p, sa, sf, f_p, f_s, mod3, conv_w, conv_b, wco, wfo, wo, g2, rwh, rwl, rb, ltri)


def _row_tile(ref, r):
    return ref.at[pl.ds(pl.multiple_of(r * ROW_TILE, ROW_TILE), ROW_TILE), :]


def _dispatch_body(dest_ref, h_ref, xs_hbm, sem, *, tq):
    def issue(g, c):
        for u in range(DMA_GROUP):
            t = g * DMA_GROUP + u
            for k in range(TOPK):
                pltpu.make_async_copy(_row_tile(h_ref, t), _row_tile(xs_hbm, dest_ref[k, t]),
                                      sem).start(priority=(u * TOPK + k) % 2)
        return c

    lax.fori_loop(0, tq // DMA_GROUP, issue, 0)
    n = tq * TOPK * ROW_TILE
    pltpu.make_async_copy(xs_hbm.at[pl.ds(0, n), :], xs_hbm.at[pl.ds(0, n), :], sem).wait()


def _dispatch(tq, dest, h2r, n_rows):
    tt = h2r.shape[0] // ROW_TILE
    return pl.pallas_call(
        functools.partial(_dispatch_body, tq=tq),
        grid=(tt // tq,),
        in_specs=[pl.BlockSpec((TOPK, tq), lambda i: (0, i), memory_space=pltpu.SMEM),
                  pl.BlockSpec((tq * ROW_TILE, LANES), lambda i: (i, 0))],
        out_specs=pl.BlockSpec(memory_space=pl.ANY),
        out_shape=jax.ShapeDtypeStruct((n_rows * ROW_TILE, LANES), F32),
        scratch_shapes=[pltpu.SemaphoreType.DMA],
        compiler_params=_params(("arbitrary",)),
        name="moe_dispatch",
    )(dest, h2r)


def _ffn_body(be_ref, nv_ref, nu_ref, xs_ref, wgu_ref, bgu_ref, wdn_ref, bdn_ref, os_ref,
              wgu_bf, wdn_bf, *, bm):
    i = pl.program_id(0)

    @pl.when(i < nu_ref[0])
    def _():
        @pl.when((i == 0) | (be_ref[i] != be_ref[jnp.maximum(i - 1, 0)]))
        def _():
            wgu_bf[...] = wgu_ref[...].astype(BF16)
            wdn_bf[...] = wdn_ref[...].astype(BF16)

        x = _from_row_tiles(xs_ref, 0, bm)
        valid = lax.broadcasted_iota(I32, (bm, 1), 0) < nv_ref[i]
        x = jnp.where(valid, x, 0.0).astype(BF16)
        gu = jnp.dot(x, wgu_bf[...], preferred_element_type=F32) + bgu_ref[...]
        glu = jnp.minimum(gu[:, :DFF], LIMIT)
        lin = jnp.clip(gu[:, DFF:], -LIMIT, LIMIT)
        act = glu * jax.nn.sigmoid(ALPHA * glu) * (lin + 1.0)
        out = jnp.dot(act.astype(BF16), wdn_bf[...], preferred_element_type=F32) + bdn_ref[...]
        _to_row_tiles(os_ref, out, bm)


def _expert_ffn(bm, block_expert, n_valid, n_used, xs, wgu, bgu, wdn, bdn):
    n_blocks = xs.shape[0] // (bm * ROW_TILE)
    rows = lambda i, be, nv, nu: (jnp.minimum(i, nu[0] - 1), 0)
    wsel = lambda i, be, nv, nu: (be[i], 0, 0)
    grid_spec = pltpu.PrefetchScalarGridSpec(
        num_scalar_prefetch=3,
        grid=(n_blocks,),
        in_specs=[pl.BlockSpec((bm * ROW_TILE, LANES), rows),
                  pl.BlockSpec((None, D, 2 * DFF), wsel),
                  pl.BlockSpec((None, 1, 2 * DFF), wsel),
                  pl.BlockSpec((None, DFF, D), wsel),
                  pl.BlockSpec((None, 1, D), wsel)],
        out_specs=pl.BlockSpec((bm * ROW_TILE, LANES), rows),
        scratch_shapes=[pltpu.VMEM((D, 2 * DFF), BF16), pltpu.VMEM((DFF, D), BF16)],
    )
    return pl.pallas_call(
        functools.partial(_ffn_body, bm=bm),
        grid_spec=grid_spec,
        out_shape=jax.ShapeDtypeStruct(xs.shape, F32),
        compiler_params=_params(("arbitrary",)),
        name="expert_ffn",
    )(block_expert, n_valid, n_used, xs, wgu, bgu, wdn, bdn)


def _combine_body(dcur_ref, dnxt_ref, os_hbm, w_ref, x1_ref, mod_ref, g_ref, *rest, tq, final, npt):
    out_refs, (stage, sems) = rest[:-2], rest[-2:]
    i = pl.program_id(0)
    n = pl.num_programs(0)

    def issue(dref, slot):
        def grp(g, c):
            for u in range(DMA_GROUP):
                t = g * DMA_GROUP + u
                for k in range(TOPK):
                    dst = stage.at[slot, pl.ds(pl.multiple_of((k * tq + t) * ROW_TILE, ROW_TILE), ROW_TILE), :]
                    pltpu.make_async_copy(_row_tile(os_hbm, dref[k, t]), dst,
                                          sems.at[slot]).start(priority=(u * TOPK + k) % 2)
            return c
        lax.fori_loop(0, tq // DMA_GROUP, grp, 0)

    @pl.when(i == 0)
    def _():
        issue(dcur_ref, 0)

    @pl.when(i + 1 < n)
    def _():
        issue(dnxt_ref, (i + 1) % 2)

    slot = i % 2
    nrow = TOPK * tq * ROW_TILE
    pltpu.make_async_copy(os_hbm.at[pl.ds(0, nrow), :], stage.at[slot], sems.at[slot]).wait()
    buf = stage.at[slot]
    y = w_ref[:, 0:1] * _from_row_tiles(buf, 0, tq)
    for k in range(1, TOPK):
        y = y + w_ref[:, k:k + 1] * _from_row_tiles(buf, k * tq * ROW_TILE, tq)
    x2 = x1_ref[...] + mod_ref[:, 5 * D:6 * D] * y
    if not final:
        out_refs[0][...] = x2
    else:
        ms = jnp.mean(x2 * x2, axis=-1, keepdims=True)
        x2 = x2 * lax.rsqrt(ms + EPS) * g_ref[...]

        @pl.when(i < npt)
        def _():
            out_refs[0][...] = x2

        @pl.when(i >= npt)
        def _():
            out_refs[1][...] = x2


def _combine(geom, tq, dest, os_rows, wts, x1, mod3, g_final, final):
    tt = geom.tt
    nt = tt // tq
    if final:
        out_specs = geom.split_specs(tq, D)
        out_shape = [jax.ShapeDtypeStruct((geom.tp, D), F32), jax.ShapeDtypeStruct((geom.ts, D), F32)]
    else:
        out_specs = [pl.BlockSpec((tq, D), lambda i: (i, 0))]
        out_shape = [jax.ShapeDtypeStruct((tt, D), F32)]
    return pl.pallas_call(
        functools.partial(_combine_body, tq=tq, final=final, npt=geom.tp // tq),
        grid=(nt,),
        in_specs=[pl.BlockSpec((TOPK, tq), lambda i: (0, i), memory_space=pltpu.SMEM),
                  pl.BlockSpec((TOPK, tq), lambda i: (0, jnp.minimum(i + 1, nt - 1)),
                               memory_space=pltpu.SMEM),
                  pl.BlockSpec(memory_space=pl.ANY),
                  pl.BlockSpec((tq, LANES), lambda i: (i, 0)),
                  pl.BlockSpec((tq, D), lambda i: (i, 0)),
                  pl.BlockSpec((None, 1, N_MOD * D), lambda i: (geom.seq_of(i * tq), 0, 0)),
                  pl.BlockSpec((1, D), lambda i: (0, 0))],
        out_specs=out_specs,
        out_shape=out_shape,
        scratch_shapes=[pltpu.VMEM((2, TOPK * tq * ROW_TILE, LANES), F32),
                        pltpu.SemaphoreType.DMA((2,))],
        compiler_params=_params(("arbitrary",)),
        name="moe_combine",
    )(dest, dest, os_rows, wts, x1, mod3, g_final)


def _route_plan(meta_t, counts_f, bm, n_blocks):
    idx = meta_t[0:TOPK]
    rank = meta_t[TOPK:2 * TOPK]
    counts = counts_f[0, :E].astype(I32)
    padded = (counts + bm - 1) // bm * bm
    pad_end = jnp.cumsum(padded)
    pad_start = pad_end - padded
    dest = pad_start[idx] + rank
    n_used = (pad_end[E - 1] // bm).astype(I32)
    blk = jnp.arange(n_blocks, dtype=I32)
    be = jnp.minimum(jnp.sum((pad_end[None, :] <= (blk * bm)[:, None]).astype(I32), axis=1), E - 1)
    be = jnp.where(blk < n_used, be, be[n_used - 1]).astype(I32)
    n_valid = jnp.clip((pad_start + counts)[be] - blk * bm, 0, bm).astype(I32)
    return dest, be, n_valid, n_used.reshape(1)


def _trunk(geom, x_p, x_s, c_all, mod_w, mod_b, norm1_g, norm2_g, w_in, conv_w, conv_b, w_conv_out,
           w_fourier_out, w_o, router_w, router_b, w_gate_up, b_gate_up, w_down, b_down, final_g,
           tm=512, tq=256, bm=512):
    depth = mod_w.shape[0]
    tt = geom.tt
    n_blocks = -(-tt * TOPK // bm) + E
    n_rows = n_blocks * bm
    cat = jnp.asarray(_chan_dft(), BF16)
    ltri = jnp.asarray(np.tril(np.ones((tm, tm), np.float32), -1), BF16)
    c_pad = jnp.zeros((-(-geom.nseq // 8) * 8, D), F32).at[:geom.nseq].set(c_all)
    xs = (x_p, x_s)
    for l in range(depth):
        mod = _modulation(c_pad, mod_w[l], mod_b[l])
        mod3 = mod.reshape(mod.shape[0], 1, N_MOD * D)
        gb, p, sa, sf, a_p, b_p, a_s, b_s = _inproj(geom, tm, xs, mod3, norm1_g[l].reshape(1, D),
                                                    w_in[l].astype(BF16), cat)
        f_p = _fourier(a_p, b_p, geom.bp, geom.sp)
        f_s = _fourier(a_s, b_s, geom.bs, geom.ss)
        rw = jnp.zeros((D, LANES), F32).at[:, :E].set(router_w[l])
        rwh = rw.astype(BF16)
        rwl = (rw - rwh.astype(F32)).astype(BF16)
        rb = jnp.full((1, LANES), NEG, F32).at[0, :E].set(router_b[l])
        x1, h2r, meta_t, wts, counts = _mixer(
            geom, tm, xs, gb, p, sa, sf, f_p, f_s, mod3, conv_w[l], conv_b[l].reshape(1, D),
            w_conv_out[l].astype(BF16), w_fourier_out[l].astype(BF16), w_o[l].astype(BF16),
            norm2_g[l].reshape(1, D), rwh, rwl, rb, ltri)
        dest, block_expert, n_valid, n_used = _route_plan(meta_t, counts, bm, n_blocks)
        xrows = _dispatch(tq, dest, h2r, n_rows)
        os_rows = _expert_ffn(bm, block_expert, n_valid, n_used, xrows,
                              w_gate_up[l], b_gate_up[l].reshape(E, 1, 2 * DFF),
                              w_down[l], b_down[l].reshape(E, 1, D))
        xs = tuple(_combine(geom, tq, dest, os_rows, wts, x1, mod3, final_g.reshape(1, D),
                            final=(l == depth - 1)))
    return xs


def kernel(x_prompt, x_sample, c_prompt, c_sample, mod_w, mod_b, norm1_g, norm2_g, w_in, conv_w,
           conv_b, w_conv_out, w_fourier_out, w_o, router_w, router_b, w_gate_up, b_gate_up,
           w_down, b_down, final_g):
    bp, sp, _ = x_prompt.shape
    bs, ss, _ = x_sample.shape
    geom = _Geom(bp, sp, bs, ss)
    c_all = jnp.concatenate([c_prompt, c_sample], axis=0)
    y_p, y_s = _trunk(geom, x_prompt.reshape(bp * sp, D), x_sample.reshape(bs * ss, D), c_all,
                      mod_w, mod_b, norm1_g, norm2_g, w_in, conv_w, conv_b, w_conv_out,
                      w_fourier_out, w_o, router_w, router_b, w_gate_up, b_gate_up, w_down, b_down,
                      final_g)
    return (y_p.reshape(bp, sp, D), y_s.reshape(bs, ss, D))
```

```python
import functools
import math

import numpy as np
import jax
import jax.numpy as jnp
from jax import lax
from jax.experimental import pallas as pl
from jax.experimental.pallas import tpu as pltpu

F32 = jnp.float32
BF16 = jnp.bfloat16
I32 = jnp.int32

D = 1024
DF = 512
NG = 4
DG = DF // NG
E = 32
TOPK = 4
DFF = 1024
N_MOD = 6
ALPHA = 1.702
LIMIT = 7.0
EPS = 1e-5
LANES = 128
SUBLANES = 8
ROW_TILE = D // LANES
FFT_N1 = 128
FFT_K1_TILE = SUBLANES
VMEM_LIMIT = 56 * 1024 * 1024
NEG = -1e30
DMA_GROUP = 8

assert ROW_TILE == SUBLANES


def _params(sem, vmem=VMEM_LIMIT):
    return pltpu.CompilerParams(dimension_semantics=sem, vmem_limit_bytes=vmem)


class _Geom:
    def __init__(self, bp, sp, bs, ss):
        assert bp > 0 and bs > 0
        self.bp, self.sp, self.bs, self.ss = bp, sp, bs, ss
        self.tp = bp * sp
        self.ts = bs * ss
        self.tt = self.tp + self.ts
        self.nseq = bp + bs

    def seq_of(self, t0):
        return jnp.where(t0 < self.tp, t0 // self.sp, self.bp + (t0 - self.tp) // self.ss)

    def offset_len(self, t0):
        in_p = t0 < self.tp
        off = jnp.where(in_p, t0 % self.sp, (t0 - self.tp) % self.ss)
        slen = jnp.where(in_p, self.sp, self.ss)
        return off, slen

    def split_specs(self, tm, width):
        npt = self.tp // tm
        return [pl.BlockSpec((tm, width), lambda i: (jnp.minimum(i, npt - 1), 0)),
                pl.BlockSpec((tm, width), lambda i: (jnp.maximum(i - npt, 0), 0))]


def _to_row_tiles(ref, val, rows):
    for s in range(ROW_TILE):
        ref[pl.ds(s, rows, stride=ROW_TILE), :] = val[:, s * LANES:(s + 1) * LANES]


def _from_row_tiles(ref, start, rows):
    return jnp.concatenate(
        [ref[pl.ds(start + s, rows, stride=ROW_TILE), :] for s in range(ROW_TILE)], axis=1)


def _mod_body(c_ref, w_ref, b_ref, o_ref):
    c = c_ref[...]
    s = c * jax.nn.sigmoid(c)
    o_ref[...] = jnp.dot(s, w_ref[...], preferred_element_type=F32,
                         precision=lax.Precision.HIGHEST) + b_ref[...]


def _modulation(c_pad, mod_w, mod_b):
    rows = c_pad.shape[0]
    return pl.pallas_call(
        _mod_body,
        grid=(N_MOD,),
        in_specs=[pl.BlockSpec((rows, D), lambda j: (0, 0)),
                  pl.BlockSpec((D, D), lambda j: (0, j)),
                  pl.BlockSpec((1, D), lambda j: (0, j))],
        out_specs=pl.BlockSpec((rows, D), lambda j: (0, j)),
        out_shape=jax.ShapeDtypeStruct((rows, N_MOD * D), F32),
        compiler_params=_params(("parallel",)),
        name="modulation",
    )(c_pad, mod_w, mod_b.reshape(1, N_MOD * D))


def _inproj_body(*refs, n_x, npt):
    x_refs = refs[:n_x]
    (mod_ref, g_ref, w_ref, cat_ref,
     gb_ref, p_ref, sa_ref, sf_ref, ap_ref, bp_ref, as_ref, bs_ref) = refs[n_x:]
    is_p = pl.program_id(0) < npt
    x = x_refs[0][...] if n_x == 1 else jnp.where(is_p, x_refs[0][...], x_refs[1][...])
    ms = jnp.mean(x * x, axis=-1, keepdims=True)
    xn = x * lax.rsqrt(ms + EPS) * g_ref[...]
    shift = mod_ref[:, 0:D]
    scale = mod_ref[:, D:2 * D]
    h = (xn * (1.0 + scale) + shift).astype(BF16)

    def proj(lo, hi):
        return jnp.dot(h, w_ref[:, lo:hi], preferred_element_type=F32)

    gb_ref[...] = proj(0, D).astype(BF16)
    p_ref[...] = (proj(D, 2 * D) * proj(2 * D, 3 * D)).astype(BF16)
    uf = proj(3 * D, 3 * D + DF).astype(BF16)
    abs_ = [jnp.dot(uf[:, g * DG:(g + 1) * DG], cat_ref[...], preferred_element_type=F32)
            for g in range(NG)]
    a = jnp.concatenate([ab[:, :DG] for ab in abs_], axis=1).astype(BF16)
    b = jnp.concatenate([ab[:, DG:] for ab in abs_], axis=1).astype(BF16)

    @pl.when(is_p)
    def _():
        ap_ref[...] = a
        bp_ref[...] = b

    @pl.when(jnp.logical_not(is_p))
    def _():
        as_ref[...] = a
        bs_ref[...] = b

    sa_ref[...] = jax.nn.sigmoid(proj(3 * D + DF, 4 * D + DF)).astype(BF16)
    sf_ref[...] = jax.nn.sigmoid(proj(4 * D + DF, 5 * D + DF)).astype(BF16)


def _inproj(geom, tm, xs, mod3, g1, w_in_bf, cat):
    tt = geom.tt
    npt = geom.tp // tm
    row = lambda i: (i, 0)
    full = lambda i: (0, 0)
    bf = lambda r, n: jax.ShapeDtypeStruct((r, n), BF16)
    x_specs = geom.split_specs(tm, D) if len(xs) == 2 else [pl.BlockSpec((tm, D), row)]
    ab_specs = geom.split_specs(tm, DF)
    return pl.pallas_call(
        functools.partial(_inproj_body, n_x=len(xs), npt=npt),
        grid=(tt // tm,),
        in_specs=x_specs + [
            pl.BlockSpec((None, 1, N_MOD * D), lambda i: (geom.seq_of(i * tm), 0, 0)),
            pl.BlockSpec((1, D), full),
            pl.BlockSpec(w_in_bf.shape, full),
            pl.BlockSpec(cat.shape, full)],
        out_specs=[pl.BlockSpec((tm, D), row), pl.BlockSpec((tm, D), row),
                   pl.BlockSpec((tm, D), row), pl.BlockSpec((tm, D), row),
                   ab_specs[0], ab_specs[0], ab_specs[1], ab_specs[1]],
        out_shape=[bf(tt, D), bf(tt, D), bf(tt, D), bf(tt, D),
                   bf(geom.tp, DF), bf(geom.tp, DF), bf(geom.ts, DF), bf(geom.ts, DF)],
        compiler_params=_params(("arbitrary",)),
        name="inproj",
    )(*xs, mod3, g1, w_in_bf, cat)


@functools.lru_cache(maxsize=None)
def _fft_consts(s):
    n1 = FFT_N1
    n2 = s // n1
    a = np.arange(n1)
    ang = 2.0 * np.pi * (np.outer(a, a) % n1) / n1
    c1, s1 = np.cos(ang), np.sin(ang)
    w1 = np.block([[c1, s1], [-s1, c1]])
    ang = 2.0 * np.pi * (np.outer(np.arange(n1), np.arange(n2)) % s) / s
    tc = np.repeat(np.cos(ang), LANES, axis=1)
    ts = np.repeat(np.sin(ang), LANES, axis=1)
    m = np.arange(n2)
    ang = 2.0 * np.pi * (np.outer(m, m) % n2) / n2
    c2, s2 = np.cos(ang), np.sin(ang)
    kt = FFT_K1_TILE
    wexp = np.zeros((n2, kt, 2, kt, n2))
    for j in range(kt):
        wexp[:, j, 0, j, :] = c2
        wexp[:, j, 1, j, :] = s2
    wexp = wexp.reshape(n2 * kt, 2 * kt * n2)
    return (np.asarray(w1, np.float32), np.asarray(tc, np.float32), np.asarray(ts, np.float32),
            np.asarray(wexp, np.float32))


@functools.lru_cache(maxsize=None)
def _chan_dft():
    a = np.arange(DG)
    ang = 2.0 * np.pi * (np.outer(a, a) % DG) / DG
    return np.asarray(np.concatenate([np.cos(ang), -np.sin(ang)], axis=1), np.float32)


def _fft1_body(a_ref, b_ref, w_ref, tc_ref, ts_ref, yr_ref, yi_ref, *, tn2):
    ab = jnp.concatenate([a_ref[...], b_ref[...]], axis=0)
    y = jnp.dot(w_ref[...], ab, preferred_element_type=F32)
    n1 = FFT_N1
    for j in range(tn2):
        c = tc_ref[:, j * LANES:(j + 1) * LANES]
        s = ts_ref[:, j * LANES:(j + 1) * LANES]
        for g in range(NG):
            lo = (j * NG + g) * LANES
            yr = y[:n1, lo:lo + LANES]
            yi = y[n1:, lo:lo + LANES]
            yr_ref[:, lo:lo + LANES] = (yr * c + yi * s).astype(BF16)
            yi_ref[:, lo:lo + LANES] = (yi * c - yr * s).astype(BF16)


def _fft2_body(yr_ref, yi_ref, w_ref, f_ref, *, n2, scale):
    rhs = jnp.concatenate([yr_ref[...], yi_ref[...]], axis=0)
    z = jnp.dot(w_ref[...], rhs, preferred_element_type=F32) * scale
    f_ref[...] = z.reshape(n2, FFT_K1_TILE, DF)


def _fourier(a, b, nseq, s):
    n1 = FFT_N1
    n2 = s // n1
    tn2 = 8
    kt = FFT_K1_TILE
    w1, tc, ts, wexp = _fft_consts(s)
    a3 = a.reshape(nseq, n1, n2 * DF)
    b3 = b.reshape(nseq, n1, n2 * DF)
    blk = lambda q, j: (q, 0, j)
    yr, yi = pl.pallas_call(
        functools.partial(_fft1_body, tn2=tn2),
        grid=(nseq, n2 // tn2),
        in_specs=[pl.BlockSpec((None, n1, tn2 * DF), blk),
                  pl.BlockSpec((None, n1, tn2 * DF), blk),
                  pl.BlockSpec((2 * n1, 2 * n1), lambda q, j: (0, 0)),
                  pl.BlockSpec((n1, tn2 * LANES), lambda q, j: (0, j)),
                  pl.BlockSpec((n1, tn2 * LANES), lambda q, j: (0, j))],
        out_specs=[pl.BlockSpec((None, n1, tn2 * DF), blk),
                   pl.BlockSpec((None, n1, tn2 * DF), blk)],
        out_shape=[jax.ShapeDtypeStruct((nseq, n1, n2 * DF), BF16)] * 2,
        compiler_params=_params(("parallel", "parallel")),
        name="fft_stage1",
    )(a3, b3, jnp.asarray(w1, BF16), jnp.asarray(tc), jnp.asarray(ts))
    yr = yr.reshape(nseq, n1 * n2, DF)
    yi = yi.reshape(nseq, n1 * n2, DF)
    f = pl.pallas_call(
        functools.partial(_fft2_body, n2=n2, scale=1.0 / math.sqrt(s * DG)),
        grid=(nseq, n1 // kt),
        in_specs=[pl.BlockSpec((None, kt * n2, DF), lambda q, j: (q, j, 0)),
                  pl.BlockSpec((None, kt * n2, DF), lambda q, j: (q, j, 0)),
                  pl.BlockSpec(wexp.shape, lambda q, j: (0, 0))],
        out_specs=pl.BlockSpec((None, n2, kt, DF), lambda q, j: (q, 0, j, 0)),
        out_shape=jax.ShapeDtypeStruct((nseq, n2, n1, DF), F32),
        compiler_params=_params(("parallel", "parallel")),
        name="fft_stage2",
    )(yr, yi, jnp.asarray(wexp, BF16))
    return f.reshape(nseq * s, DF)


def _mixer_body(*refs, n_x, npt, tm, geom, halo):
    x_refs = refs[:n_x]
    (gb_ref, p_ref, pprev_ref, pnext_ref, sa_ref, sf_ref, fp_ref, fs_ref, mod_ref,
     cw_ref, cb_ref, wco_ref, wfo_ref, wo_ref, g2_ref, rwh_ref, rwl_ref, rb_ref, ltri_ref,
     x1_ref, h2_ref, meta_ref, wts_ref, cnt_ref, carry) = refs[n_x:]
    i = pl.program_id(0)
    is_p = i < npt

    @pl.when(i == 0)
    def _():
        carry[...] = jnp.zeros_like(carry)

    x = x_refs[0][...] if n_x == 1 else jnp.where(is_p, x_refs[0][...], x_refs[1][...])
    f = jnp.where(is_p, fp_ref[...], fs_ref[...])

    off, slen = geom.offset_len(i * tm)
    keep_prev = jnp.where(off == 0, 0.0, 1.0)
    keep_next = jnp.where(off + tm == slen, 0.0, 1.0)
    p = p_ref[...].astype(F32)
    prev_row = pprev_ref[halo - 1:halo, :].astype(F32) * keep_prev
    next_row = pnext_ref[0:1, :].astype(F32) * keep_next
    row = lax.broadcasted_iota(I32, (tm, 1), 0)
    pm = jnp.where(row == 0, prev_row, pltpu.roll(p, 1, 0))
    pp = jnp.where(row == tm - 1, next_row, pltpu.roll(p, tm - 1, 0))
    conv = cw_ref[0:1, :] * pm + cw_ref[1:2, :] * p + cw_ref[2:3, :] * pp + cb_ref[...]
    ya_in = (gb_ref[...].astype(F32) * conv).astype(BF16)
    ya = jnp.dot(ya_in, wco_ref[...], preferred_element_type=F32)
    yf = jnp.dot(f.astype(BF16), wfo_ref[...], preferred_element_type=F32)
    merged = sa_ref[...].astype(F32) * ya + sf_ref[...].astype(F32) * yf
    mix = jnp.dot(merged.astype(BF16), wo_ref[...], preferred_element_type=F32)
    x1 = x + mod_ref[:, 2 * D:3 * D] * mix
    x1_ref[...] = x1

    ms = jnp.mean(x1 * x1, axis=-1, keepdims=True)
    xn = x1 * lax.rsqrt(ms + EPS) * g2_ref[...]
    h2 = xn * (1.0 + mod_ref[:, 4 * D:5 * D]) + mod_ref[:, 3 * D:4 * D]
    _to_row_tiles(h2_ref, h2, tm)

    hh = h2.astype(BF16)
    hl = (h2 - hh.astype(F32)).astype(BF16)
    logits = (jnp.dot(hh, rwh_ref[...], preferred_element_type=F32)
              + jnp.dot(hh, rwl_ref[...], preferred_element_type=F32)
              + jnp.dot(hl, rwh_ref[...], preferred_element_type=F32)) + rb_ref[...]

    lane = lax.broadcasted_iota(I32, (tm, LANES), 1)
    l = logits
    vals, idxs = [], []
    for _ in range(TOPK):
        m = jnp.max(l, axis=-1, keepdims=True)
        ix = jnp.min(jnp.where(l == m, lane, LANES), axis=-1, keepdims=True)
        vals.append(m)
        idxs.append(ix)
        l = jnp.where(lane == ix, -jnp.inf, l)
    exps = [jnp.exp(v - vals[0]) for v in vals]
    denom = exps[0] + exps[1] + exps[2] + exps[3]
    ws = [e / denom for e in exps]

    onehot = jnp.zeros((tm, LANES), F32)
    for ix in idxs:
        onehot = onehot + jnp.where(lane == ix, 1.0, 0.0)
    before = jnp.dot(ltri_ref[...], onehot.astype(BF16), preferred_element_type=F32) + carry[0:1, :]
    ranks = [jnp.sum(jnp.where(lane == ix, before, 0.0), axis=-1, keepdims=True).astype(I32)
             for ix in idxs]
    carry[...] = carry[...] + jnp.sum(onehot, axis=0, keepdims=True)
    cnt_ref[...] = carry[...]

    meta = jnp.zeros((tm, LANES), I32)
    wts = jnp.zeros((tm, LANES), F32)
    for k in range(TOPK):
        meta = jnp.where(lane == k, idxs[k], meta)
        meta = jnp.where(lane == TOPK + k, ranks[k], meta)
        wts = jnp.where(lane == k, ws[k], wts)
    meta_ref[...] = jnp.transpose(meta)[0:2 * TOPK, :]
    wts_ref[...] = wts


def _mixer(geom, tm, xs, gb, p, sa, sf, f_p, f_s, mod3, conv_w, conv_b, wco, wfo, wo, g2,
           rwh, rwl, rb, ltri):
    tt = geom.tt
    halo = 16
    hb = tm // halo
    nhalo = tt // halo
    row = lambda i: (i, 0)
    full = lambda i: (0, 0)
    x_specs = geom.split_specs(tm, D) if len(xs) == 2 else [pl.BlockSpec((tm, D), row)]
    body = functools.partial(_mixer_body, n_x=len(xs), npt=geom.tp // tm, tm=tm, geom=geom, halo=halo)
    return pl.pallas_call(
        body,
        grid=(tt // tm,),
        in_specs=x_specs + [
            pl.BlockSpec((tm, D), row),
            pl.BlockSpec((tm, D), row),
            pl.BlockSpec((halo, D), lambda i: (jnp.maximum(i * hb - 1, 0), 0)),
            pl.BlockSpec((halo, D), lambda i: (jnp.minimum((i + 1) * hb, nhalo - 1), 0)),
            pl.BlockSpec((tm, D), row),
            pl.BlockSpec((tm, D), row)] + geom.split_specs(tm, DF) + [
            pl.BlockSpec((None, 1, N_MOD * D), lambda i: (geom.seq_of(i * tm), 0, 0)),
            pl.BlockSpec((3, D), full),
            pl.BlockSpec((1, D), full),
            pl.BlockSpec((D, D), full),
            pl.BlockSpec((DF, D), full),
            pl.BlockSpec((D, D), full),
            pl.BlockSpec((1, D), full),
            pl.BlockSpec((D, LANES), full),
            pl.BlockSpec((D, LANES), full),
            pl.BlockSpec((1, LANES), full),
            pl.BlockSpec((tm, tm), full)],
        out_specs=[pl.BlockSpec((tm, D), row),
                   pl.BlockSpec((tm * ROW_TILE, LANES), row),
                   pl.BlockSpec((2 * TOPK, tm), lambda i: (0, i)),
                   pl.BlockSpec((tm, LANES), row),
                   pl.BlockSpec((8, LANES), full)],
        out_shape=[jax.ShapeDtypeStruct((tt, D), F32),
                   jax.ShapeDtypeStruct((tt * ROW_TILE, LANES), F32),
                   jax.ShapeDtypeStruct((2 * TOPK, tt), I32),
                   jax.ShapeDtypeStruct((tt, LANES), F32),
                   jax.ShapeDtypeStruct((8, LANES), F32)],
        scratch_shapes=[pltpu.VMEM((8, LANES), F32)],
        compiler_params=_params(("arbitrary",)),
        name="mixer_router",
    )(*xs, gb, p, p, p, sa, sf, f_p, f_s, mod3, conv_w, conv_b, wco, wfo, wo, g2, rwh, rwl, rb, ltri)


def _row_tile(ref, r):
    return ref.at[pl.ds(pl.multiple_of(r * ROW_TILE, ROW_TILE), ROW_TILE), :]


def _dispatch_body(dest_ref, h_ref, xs_hbm, sem, *, tq):
    def issue(g, c):
        for u in range(DMA_GROUP):
            t = g * DMA_GROUP + u
            for k in range(TOPK):
                pltpu.make_async_copy(_row_tile(h_ref, t), _row_tile(xs_hbm, dest_ref[k, t]),
                                      sem).start(priority=(u * TOPK + k) % 2)
        return c

    lax.fori_loop(0, tq // DMA_GROUP, issue, 0)
    n = tq * TOPK * ROW_TILE
    pltpu.make_async_copy(xs_hbm.at[pl.ds(0, n), :], xs_hbm.at[pl.ds(0, n), :], sem).wait()


def _dispatch(tq, dest, h2r, n_rows):
    tt = h2r.shape[0] // ROW_TILE
    return pl.pallas_call(
        functools.partial(_dispatch_body, tq=tq),
        grid=(tt // tq,),
        in_specs=[pl.BlockSpec((TOPK, tq), lambda i: (0, i), memory_space=pltpu.SMEM),
                  pl.BlockSpec((tq * ROW_TILE, LANES), lambda i: (i, 0))],
        out_specs=pl.BlockSpec(memory_space=pl.ANY),
        out_shape=jax.ShapeDtypeStruct((n_rows * ROW_TILE, LANES), F32),
        scratch_shapes=[pltpu.SemaphoreType.DMA],
        compiler_params=_params(("arbitrary",)),
        name="moe_dispatch",
    )(dest, h2r)


def _ffn_body(be_ref, nv_ref, nu_ref, xs_ref, wgu_ref, bgu_ref, wdn_ref, bdn_ref, os_ref,
              wgu_bf, wdn_bf, *, bm):
    i = pl.program_id(0)

    @pl.when(i < nu_ref[0])
    def _():
        @pl.when((i == 0) | (be_ref[i] != be_ref[jnp.maximum(i - 1, 0)]))
        def _():
            wgu_bf[...] = wgu_ref[...].astype(BF16)
            wdn_bf[...] = wdn_ref[...].astype(BF16)

        x = _from_row_tiles(xs_ref, 0, bm)
        valid = lax.broadcasted_iota(I32, (bm, 1), 0) < nv_ref[i]
        x = jnp.where(valid, x, 0.0).astype(BF16)
        gu = jnp.dot(x, wgu_bf[...], preferred_element_type=F32) + bgu_ref[...]
        glu = jnp.minimum(gu[:, :DFF], LIMIT)
        lin = jnp.clip(gu[:, DFF:], -LIMIT, LIMIT)
        act = glu * jax.nn.sigmoid(ALPHA * glu) * (lin + 1.0)
        out = jnp.dot(act.astype(BF16), wdn_bf[...], preferred_element_type=F32) + bdn_ref[...]
        _to_row_tiles(os_ref, out, bm)


def _expert_ffn(bm, layer, block_expert, n_valid, n_used, xs, wgu, bgu, wdn, bdn):
    n_blocks = xs.shape[0] // (bm * ROW_TILE)
    rows = lambda i, be, nv, nu: (jnp.minimum(i, nu[0] - 1), 0)
    wsel = lambda i, be, nv, nu: (layer, be[i], 0, 0)
    grid_spec = pltpu.PrefetchScalarGridSpec(
        num_scalar_prefetch=3,
        grid=(n_blocks,),
        in_specs=[pl.BlockSpec((bm * ROW_TILE, LANES), rows),
                  pl.BlockSpec((None, None, D, 2 * DFF), wsel),
                  pl.BlockSpec((None, None, 1, 2 * DFF), wsel),
                  pl.BlockSpec((None, None, DFF, D), wsel),
                  pl.BlockSpec((None, None, 1, D), wsel)],
        out_specs=pl.BlockSpec((bm * ROW_TILE, LANES), rows),
        scratch_shapes=[pltpu.VMEM((D, 2 * DFF), BF16), pltpu.VMEM((DFF, D), BF16)],
    )
    return pl.pallas_call(
        functools.partial(_ffn_body, bm=bm),
        grid_spec=grid_spec,
        out_shape=jax.ShapeDtypeStruct(xs.shape, F32),
        compiler_params=_params(("arbitrary",)),
        name="expert_ffn",
    )(block_expert, n_valid, n_used, xs, wgu, bgu, wdn, bdn)


def _combine_body(dcur_ref, dnxt_ref, os_hbm, w_ref, x1_ref, mod_ref, g_ref, *rest, tq, final, npt):
    out_refs, (stage, sems) = rest[:-2], rest[-2:]
    i = pl.program_id(0)
    n = pl.num_programs(0)

    def issue(dref, slot):
        def grp(g, c):
            for u in range(DMA_GROUP):
                t = g * DMA_GROUP + u
                for k in range(TOPK):
                    dst = stage.at[slot, pl.ds(pl.multiple_of((k * tq + t) * ROW_TILE, ROW_TILE), ROW_TILE), :]
                    pltpu.make_async_copy(_row_tile(os_hbm, dref[k, t]), dst,
                                          sems.at[slot]).start(priority=(u * TOPK + k) % 2)
            return c
        lax.fori_loop(0, tq // DMA_GROUP, grp, 0)

    @pl.when(i == 0)
    def _():
        issue(dcur_ref, 0)

    @pl.when(i + 1 < n)
    def _():
        issue(dnxt_ref, (i + 1) % 2)

    slot = i % 2
    nrow = TOPK * tq * ROW_TILE
    pltpu.make_async_copy(os_hbm.at[pl.ds(0, nrow), :], stage.at[slot], sems.at[slot]).wait()
    buf = stage.at[slot]
    y = w_ref[:, 0:1] * _from_row_tiles(buf, 0, tq)
    for k in range(1, TOPK):
        y = y + w_ref[:, k:k + 1] * _from_row_tiles(buf, k * tq * ROW_TILE, tq)
    x2 = x1_ref[...] + mod_ref[:, 5 * D:6 * D] * y
    if not final:
        out_refs[0][...] = x2
    else:
        ms = jnp.mean(x2 * x2, axis=-1, keepdims=True)
        x2 = x2 * lax.rsqrt(ms + EPS) * g_ref[...]

        @pl.when(i < npt)
        def _():
            out_refs[0][...] = x2

        @pl.when(i >= npt)
        def _():
            out_refs[1][...] = x2


def _combine(geom, tq, dest, os_rows, wts, x1, mod3, g_final, final):
    tt = geom.tt
    nt = tt // tq
    if final:
        out_specs = geom.split_specs(tq, D)
        out_shape = [jax.ShapeDtypeStruct((geom.tp, D), F32), jax.ShapeDtypeStruct((geom.ts, D), F32)]
    else:
        out_specs = [pl.BlockSpec((tq, D), lambda i: (i, 0))]
        out_shape = [jax.ShapeDtypeStruct((tt, D), F32)]
    return pl.pallas_call(
        functools.partial(_combine_body, tq=tq, final=final, npt=geom.tp // tq),
        grid=(nt,),
        in_specs=[pl.BlockSpec((TOPK, tq), lambda i: (0, i), memory_space=pltpu.SMEM),
                  pl.BlockSpec((TOPK, tq), lambda i: (0, jnp.minimum(i + 1, nt - 1)),
                               memory_space=pltpu.SMEM),
                  pl.BlockSpec(memory_space=pl.ANY),
                  pl.BlockSpec((tq, LANES), lambda i: (i, 0)),
                  pl.BlockSpec((tq, D), lambda i: (i, 0)),
                  pl.BlockSpec((None, 1, N_MOD * D), lambda i: (geom.seq_of(i * tq), 0, 0)),
                  pl.BlockSpec((1, D), lambda i: (0, 0))],
        out_specs=out_specs,
        out_shape=out_shape,
        scratch_shapes=[pltpu.VMEM((2, TOPK * tq * ROW_TILE, LANES), F32),
                        pltpu.SemaphoreType.DMA((2,))],
        compiler_params=_params(("arbitrary",)),
        name="moe_combine",
    )(dest, dest, os_rows, wts, x1, mod3, g_final)


def _route_plan(meta_t, counts_f, bm, n_blocks):
    idx = meta_t[0:TOPK]
    rank = meta_t[TOPK:2 * TOPK]
    counts = counts_f[0, :E].astype(I32)
    padded = (counts + bm - 1) // bm * bm
    pad_end = jnp.cumsum(padded)
    pad_start = pad_end - padded
    dest = rank
    for e in range(E):
        dest = dest + jnp.where(idx == e, pad_start[e], 0)
    n_used = (pad_end[E - 1] // bm).astype(I32)
    blk = jnp.arange(n_blocks, dtype=I32)
    be = jnp.minimum(jnp.sum((pad_end[None, :] <= (blk * bm)[:, None]).astype(I32), axis=1), E - 1)
    be = jnp.where(blk < n_used, be, be[n_used - 1]).astype(I32)
    n_valid = jnp.clip((pad_start + counts)[be] - blk * bm, 0, bm).astype(I32)
    return dest, be, n_valid, n_used.reshape(1)


def _trunk(geom, x_p, x_s, c_all, mod_w, mod_b, norm1_g, norm2_g, w_in, conv_w, conv_b, w_conv_out,
           w_fourier_out, w_o, router_w, router_b, w_gate_up, b_gate_up, w_down, b_down, final_g,
           tm=512, tq=256, bm=512):
    depth = mod_w.shape[0]
    tt = geom.tt
    n_blocks = -(-tt * TOPK // bm) + E
    n_rows = n_blocks * bm
    cat = jnp.asarray(_chan_dft(), BF16)
    ltri = jnp.asarray(np.tril(np.ones((tm, tm), np.float32), -1), BF16)
    c_pad = jnp.zeros((-(-geom.nseq // 8) * 8, D), F32).at[:geom.nseq].set(c_all)
    xs = (x_p, x_s)
    for l in range(depth):
        mod = _modulation(c_pad, mod_w[l], mod_b[l])
        mod3 = mod.reshape(mod.shape[0], 1, N_MOD * D)
        gb, p, sa, sf, a_p, b_p, a_s, b_s = _inproj(geom, tm, xs, mod3, norm1_g[l].reshape(1, D),
                                                    w_in[l].astype(BF16), cat)
        f_p = _fourier(a_p, b_p, geom.bp, geom.sp)
        f_s = _fourier(a_s, b_s, geom.bs, geom.ss)
        rw = jnp.zeros((D, LANES), F32).at[:, :E].set(router_w[l])
        rwh = rw.astype(BF16)
        rwl = (rw - rwh.astype(F32)).astype(BF16)
        rb = jnp.full((1, LANES), NEG, F32).at[0, :E].set(router_b[l])
        x1, h2r, meta_t, wts, counts = _mixer(
            geom, tm, xs, gb, p, sa, sf, f_p, f_s, mod3, conv_w[l], conv_b[l].reshape(1, D),
            w_conv_out[l].astype(BF16), w_fourier_out[l].astype(BF16), w_o[l].astype(BF16),
            norm2_g[l].reshape(1, D), rwh, rwl, rb, ltri)
        dest, block_expert, n_valid, n_used = _route_plan(meta_t, counts, bm, n_blocks)
        xrows = _dispatch(tq, dest, h2r, n_rows)
        os_rows = _expert_ffn(bm, l, block_expert, n_valid, n_used, xrows,
                              w_gate_up, b_gate_up.reshape(depth, E, 1, 2 * DFF),
                              w_down, b_down.reshape(depth, E, 1, D))
        xs = tuple(_combine(geom, tq, dest, os_rows, wts, x1, mod3, final_g.reshape(1, D),
                            final=(l == depth - 1)))
    return xs


def kernel(x_prompt, x_sample, c_prompt, c_sample, mod_w, mod_b, norm1_g, norm2_g, w_in, conv_w,
           conv_b, w_conv_out, w_fourier_out, w_o, router_w, router_b, w_gate_up, b_gate_up,
           w_down, b_down, final_g):
    bp, sp, _ = x_prompt.shape
    bs, ss, _ = x_sample.shape
    geom = _Geom(bp, sp, bs, ss)
    c_all = jnp.concatenate([c_prompt, c_sample], axis=0)
    y_p, y_s = _trunk(geom, x_prompt.reshape(bp * sp, D), x_sample.reshape(bs * ss, D), c_all,
                      mod_w, mod_b, norm1_g, norm2_g, w_in, conv_w, conv_b, w_conv_out,
                      w_fourier_out, w_o, router_w, router_b, w_gate_up, b_gate_up, w_down, b_down,
                      final_g)
    return (y_p.reshape(bp, sp, D), y_s.reshape(bs, ss, D))
```

```python
import functools
import math

import numpy as np
import jax
import jax.numpy as jnp
from jax import lax
from jax.experimental import pallas as pl
from jax.experimental.pallas import tpu as pltpu

F32 = jnp.float32
BF16 = jnp.bfloat16
I32 = jnp.int32

D = 1024
DF = 512
NG = 4
DG = DF // NG
E = 32
TOPK = 4
DFF = 1024
N_MOD = 6
ALPHA = 1.702
LIMIT = 7.0
EPS = 1e-5
LANES = 128
SUBLANES = 8
ROW_TILE = D // LANES
FFT_N1 = 128
FFT_K1_TILE = SUBLANES
VMEM_LIMIT = 56 * 1024 * 1024
NEG = -1e30
DMA_GROUP = 8

assert ROW_TILE == SUBLANES


def _params(sem, vmem=VMEM_LIMIT):
    return pltpu.CompilerParams(dimension_semantics=sem, vmem_limit_bytes=vmem)


class _Geom:
    def __init__(self, bp, sp, bs, ss):
        assert bp > 0 and bs > 0
        self.bp, self.sp, self.bs, self.ss = bp, sp, bs, ss
        self.tp = bp * sp
        self.ts = bs * ss
        self.tt = self.tp + self.ts
        self.nseq = bp + bs

    def seq_of(self, t0):
        return jnp.where(t0 < self.tp, t0 // self.sp, self.bp + (t0 - self.tp) // self.ss)

    def offset_len(self, t0):
        in_p = t0 < self.tp
        off = jnp.where(in_p, t0 % self.sp, (t0 - self.tp) % self.ss)
        slen = jnp.where(in_p, self.sp, self.ss)
        return off, slen

    def split_specs(self, tm, width):
        npt = self.tp // tm
        return [pl.BlockSpec((tm, width), lambda i: (jnp.minimum(i, npt - 1), 0)),
                pl.BlockSpec((tm, width), lambda i: (jnp.maximum(i - npt, 0), 0))]


def _to_row_tiles(ref, val, rows):
    for s in range(ROW_TILE):
        ref[pl.ds(s, rows, stride=ROW_TILE), :] = val[:, s * LANES:(s + 1) * LANES]


def _from_row_tiles(ref, start, rows):
    return jnp.concatenate(
        [ref[pl.ds(start + s, rows, stride=ROW_TILE), :] for s in range(ROW_TILE)], axis=1)


def _mod_body(c_ref, w_ref, b_ref, o_ref):
    c = c_ref[...]
    s = c * jax.nn.sigmoid(c)
    o_ref[...] = jnp.dot(s, w_ref[...], preferred_element_type=F32,
                         precision=lax.Precision.HIGHEST) + b_ref[...]


def _modulation(c_pad, mod_w, mod_b):
    rows = c_pad.shape[0]
    return pl.pallas_call(
        _mod_body,
        grid=(N_MOD,),
        in_specs=[pl.BlockSpec((rows, D), lambda j: (0, 0)),
                  pl.BlockSpec((D, D), lambda j: (0, j)),
                  pl.BlockSpec((1, D), lambda j: (0, j))],
        out_specs=pl.BlockSpec((rows, D), lambda j: (0, j)),
        out_shape=jax.ShapeDtypeStruct((rows, N_MOD * D), F32),
        compiler_params=_params(("parallel",)),
        name="modulation",
    )(c_pad, mod_w, mod_b.reshape(1, N_MOD * D))


def _inproj_body(*refs, n_x, npt):
    x_refs = refs[:n_x]
    (mod_ref, g_ref, w_ref, cat_ref,
     gb_ref, p_ref, sa_ref, sf_ref, ap_ref, bp_ref, as_ref, bs_ref) = refs[n_x:]
    is_p = pl.program_id(0) < npt
    x = x_refs[0][...] if n_x == 1 else jnp.where(is_p, x_refs[0][...], x_refs[1][...])
    ms = jnp.mean(x * x, axis=-1, keepdims=True)
    xn = x * lax.rsqrt(ms + EPS) * g_ref[...]
    shift = mod_ref[:, 0:D]
    scale = mod_ref[:, D:2 * D]
    h = (xn * (1.0 + scale) + shift).astype(BF16)

    def proj(lo, hi):
        return jnp.dot(h, w_ref[:, lo:hi], preferred_element_type=F32)

    gb_ref[...] = proj(0, D).astype(BF16)
    p_ref[...] = (proj(D, 2 * D) * proj(2 * D, 3 * D)).astype(BF16)
    uf = proj(3 * D, 3 * D + DF).astype(BF16)
    abs_ = [jnp.dot(uf[:, g * DG:(g + 1) * DG], cat_ref[...], preferred_element_type=F32)
            for g in range(NG)]

    def put_ab(a_ref, b_ref):
        for g in range(NG):
            a_ref[g] = abs_[g][:, :DG]
            b_ref[g] = abs_[g][:, DG:]

    @pl.when(is_p)
    def _():
        put_ab(ap_ref, bp_ref)

    @pl.when(jnp.logical_not(is_p))
    def _():
        put_ab(as_ref, bs_ref)

    sa_ref[...] = jax.nn.sigmoid(proj(3 * D + DF, 4 * D + DF)).astype(BF16)
    sf_ref[...] = jax.nn.sigmoid(proj(4 * D + DF, 5 * D + DF)).astype(BF16)


def _inproj(geom, tm, xs, mod3, g1, w_in_bf, cat):
    tt = geom.tt
    npt = geom.tp // tm
    row = lambda i: (i, 0)
    full = lambda i: (0, 0)
    bf = lambda r, n: jax.ShapeDtypeStruct((r, n), BF16)
    x_specs = geom.split_specs(tm, D) if len(xs) == 2 else [pl.BlockSpec((tm, D), row)]
    ab_specs = [pl.BlockSpec((NG, tm, DG), lambda i: (0, jnp.minimum(i, npt - 1), 0)),
                pl.BlockSpec((NG, tm, DG), lambda i: (0, jnp.maximum(i - npt, 0), 0))]
    ab_shape = lambda r: jax.ShapeDtypeStruct((NG, r, DG), F32)
    return pl.pallas_call(
        functools.partial(_inproj_body, n_x=len(xs), npt=npt),
        grid=(tt // tm,),
        in_specs=x_specs + [
            pl.BlockSpec((None, 1, N_MOD * D), lambda i: (geom.seq_of(i * tm), 0, 0)),
            pl.BlockSpec((1, D), full),
            pl.BlockSpec(w_in_bf.shape, full),
            pl.BlockSpec(cat.shape, full)],
        out_specs=[pl.BlockSpec((tm, D), row), pl.BlockSpec((tm, D), row),
                   pl.BlockSpec((tm, D), row), pl.BlockSpec((tm, D), row),
                   ab_specs[0], ab_specs[0], ab_specs[1], ab_specs[1]],
        out_shape=[bf(tt, D), bf(tt, D), bf(tt, D), bf(tt, D),
                   ab_shape(geom.tp), ab_shape(geom.tp), ab_shape(geom.ts), ab_shape(geom.ts)],
        compiler_params=_params(("arbitrary",)),
        name="inproj",
    )(*xs, mod3, g1, w_in_bf, cat)


@functools.lru_cache(maxsize=None)
def _fft_consts(s):
    n1 = FFT_N1
    n2 = s // n1
    a = np.arange(n1)
    ang = 2.0 * np.pi * (np.outer(a, a) % n1) / n1
    c1, s1 = np.cos(ang), np.sin(ang)
    w1 = np.block([[c1, s1], [-s1, c1]])
    ang = 2.0 * np.pi * (np.outer(np.arange(n1), np.arange(n2)) % s) / s
    tc = np.repeat(np.cos(ang), LANES, axis=1)
    ts = np.repeat(np.sin(ang), LANES, axis=1)
    m = np.arange(n2)
    ang = 2.0 * np.pi * (np.outer(m, m) % n2) / n2
    c2, s2 = np.cos(ang), np.sin(ang)
    kt = FFT_K1_TILE
    wexp = np.zeros((n2, kt, 2, kt, n2))
    for j in range(kt):
        wexp[:, j, 0, j, :] = c2
        wexp[:, j, 1, j, :] = s2
    wexp = wexp.reshape(n2 * kt, 2 * kt * n2)
    return (np.asarray(w1, np.float32), np.asarray(tc, np.float32), np.asarray(ts, np.float32),
            np.asarray(wexp, np.float32))


@functools.lru_cache(maxsize=None)
def _chan_dft():
    a = np.arange(DG)
    ang = 2.0 * np.pi * (np.outer(a, a) % DG) / DG
    return np.asarray(np.concatenate([np.cos(ang), -np.sin(ang)], axis=1), np.float32)


def _fft1_body(a_ref, b_ref, w_ref, tc_ref, ts_ref, yr_ref, yi_ref, abuf, bbuf, yrbuf, yibuf, *, tn2):
    n1 = FFT_N1
    for g in range(NG):
        abuf[g] = a_ref[g].reshape(n1 * tn2, LANES)
        bbuf[g] = b_ref[g].reshape(n1 * tn2, LANES)

    for j in range(tn2):
        sl = pl.ds(j, n1, stride=tn2)
        a = jnp.concatenate([abuf[g, sl, :] for g in range(NG)], axis=1)
        b = jnp.concatenate([bbuf[g, sl, :] for g in range(NG)], axis=1)
        ab = jnp.concatenate([a, b], axis=0).astype(BF16)
        y = jnp.dot(w_ref[...], ab, preferred_element_type=F32)
        c = tc_ref[:, j * LANES:(j + 1) * LANES]
        s = ts_ref[:, j * LANES:(j + 1) * LANES]
        for g in range(NG):
            yr = y[:n1, g * LANES:(g + 1) * LANES]
            yi = y[n1:, g * LANES:(g + 1) * LANES]
            yrbuf[g, sl, :] = yr * c + yi * s
            yibuf[g, sl, :] = yi * c - yr * s

    for g in range(NG):
        yr_ref[g] = yrbuf[g].reshape(n1, tn2, LANES)
        yi_ref[g] = yibuf[g].reshape(n1, tn2, LANES)


def _fft2_body(yr_ref, yi_ref, w_ref, f_ref, *, n2, scale):
    yr = jnp.concatenate([yr_ref[g] for g in range(NG)], axis=1)
    yi = jnp.concatenate([yi_ref[g] for g in range(NG)], axis=1)
    rhs = jnp.concatenate([yr, yi], axis=0).astype(BF16)
    z = jnp.dot(w_ref[...], rhs, preferred_element_type=F32) * scale
    f_ref[...] = z.reshape(n2, FFT_K1_TILE, DF)


def _fourier(a, b, nseq, s):
    n1 = FFT_N1
    n2 = s // n1
    tn2 = SUBLANES
    kt = FFT_K1_TILE
    w1, tc, ts, wexp = _fft_consts(s)
    a5 = a.reshape(NG, nseq, n1, n2, DG)
    b5 = b.reshape(NG, nseq, n1, n2, DG)
    blk = pl.BlockSpec((NG, None, n1, tn2, DG), lambda q, j: (0, q, 0, j, 0))
    yr, yi = pl.pallas_call(
        functools.partial(_fft1_body, tn2=tn2),
        grid=(nseq, n2 // tn2),
        in_specs=[blk, blk,
                  pl.BlockSpec((2 * n1, 2 * n1), lambda q, j: (0, 0)),
                  pl.BlockSpec((n1, tn2 * LANES), lambda q, j: (0, j)),
                  pl.BlockSpec((n1, tn2 * LANES), lambda q, j: (0, j))],
        out_specs=[blk, blk],
        out_shape=[jax.ShapeDtypeStruct((NG, nseq, n1, n2, DG), F32)] * 2,
        scratch_shapes=[pltpu.VMEM((NG, n1 * tn2, DG), F32)] * 4,
        compiler_params=_params(("parallel", "parallel")),
        name="fft_stage1",
    )(a5, b5, jnp.asarray(w1, BF16), jnp.asarray(tc), jnp.asarray(ts))
    yr = yr.reshape(NG, nseq, n1 * n2, DG)
    yi = yi.reshape(NG, nseq, n1 * n2, DG)
    yblk = pl.BlockSpec((NG, None, kt * n2, DG), lambda q, j: (0, q, j, 0))
    f = pl.pallas_call(
        functools.partial(_fft2_body, n2=n2, scale=1.0 / math.sqrt(s * DG)),
        grid=(nseq, n1 // kt),
        in_specs=[yblk, yblk,
                  pl.BlockSpec(wexp.shape, lambda q, j: (0, 0))],
        out_specs=pl.BlockSpec((None, n2, kt, DF), lambda q, j: (q, 0, j, 0)),
        out_shape=jax.ShapeDtypeStruct((nseq, n2, n1, DF), F32),
        compiler_params=_params(("parallel", "parallel")),
        name="fft_stage2",
    )(yr, yi, jnp.asarray(wexp, BF16))
    return f.reshape(nseq * s, DF)


def _mixer_body(*refs, n_x, npt, tm, geom, halo):
    x_refs = refs[:n_x]
    (gb_ref, p_ref, pprev_ref, pnext_ref, sa_ref, sf_ref, fp_ref, fs_ref, mod_ref,
     cw_ref, cb_ref, wco_ref, wfo_ref, wo_ref, g2_ref, rw2_ref, rwh_ref, rb_ref, utri_ref,
     x1_ref, h2_ref, meta_ref, wts_ref, cnt_ref, carry) = refs[n_x:]
    i = pl.program_id(0)
    is_p = i < npt

    @pl.when(i == 0)
    def _():
        carry[...] = jnp.zeros_like(carry)

    x = x_refs[0][...] if n_x == 1 else jnp.where(is_p, x_refs[0][...], x_refs[1][...])
    f = jnp.where(is_p, fp_ref[...], fs_ref[...])

    off, slen = geom.offset_len(i * tm)
    keep_prev = jnp.where(off == 0, 0.0, 1.0)
    keep_next = jnp.where(off + tm == slen, 0.0, 1.0)
    p = p_ref[...].astype(F32)
    prev_row = pprev_ref[halo - 1:halo, :].astype(F32) * keep_prev
    next_row = pnext_ref[0:1, :].astype(F32) * keep_next
    row = lax.broadcasted_iota(I32, (tm, 1), 0)
    pm = jnp.where(row == 0, prev_row, pltpu.roll(p, 1, 0))
    pp = jnp.where(row == tm - 1, next_row, pltpu.roll(p, tm - 1, 0))
    conv = cw_ref[0:1, :] * pm + cw_ref[1:2, :] * p + cw_ref[2:3, :] * pp + cb_ref[...]
    ya_in = (gb_ref[...].astype(F32) * conv).astype(BF16)
    ya = jnp.dot(ya_in, wco_ref[...], preferred_element_type=F32)
    yf = jnp.dot(f.astype(BF16), wfo_ref[...], preferred_element_type=F32)
    merged = sa_ref[...].astype(F32) * ya + sf_ref[...].astype(F32) * yf
    mix = jnp.dot(merged.astype(BF16), wo_ref[...], preferred_element_type=F32)
    x1 = x + mod_ref[:, 2 * D:3 * D] * mix
    x1_ref[...] = x1

    ms = jnp.mean(x1 * x1, axis=-1, keepdims=True)
    xn = x1 * lax.rsqrt(ms + EPS) * g2_ref[...]
    h2 = xn * (1.0 + mod_ref[:, 4 * D:5 * D]) + mod_ref[:, 3 * D:4 * D]
    _to_row_tiles(h2_ref, h2, tm)

    hh = h2.astype(BF16)
    hl = (h2 - hh.astype(F32)).astype(BF16)
    two = jnp.dot(hh, rw2_ref[...], preferred_element_type=F32)
    logits = (two[:, :LANES] + two[:, LANES:]
              + jnp.dot(hl, rwh_ref[...], preferred_element_type=F32)) + rb_ref[...]

    l = jnp.transpose(logits)[0:E, :]
    sub = lax.broadcasted_iota(I32, (E, tm), 0)
    vals, idxs = [], []
    for _ in range(TOPK):
        m = jnp.max(l, axis=0, keepdims=True)
        ix = jnp.min(jnp.where(l == m, sub, E), axis=0, keepdims=True)
        vals.append(m)
        idxs.append(ix)
        l = jnp.where(sub == ix, -jnp.inf, l)
    exps = [jnp.exp(v - vals[0]) for v in vals]
    denom = exps[0] + exps[1] + exps[2] + exps[3]
    ws = [e / denom for e in exps]

    onehot = jnp.zeros((E, tm), F32)
    for ix in idxs:
        onehot = onehot + jnp.where(sub == ix, 1.0, 0.0)
    before = jnp.dot(onehot.astype(BF16), utri_ref[...], preferred_element_type=F32) + carry[:, 0:1]
    ranks = [jnp.sum(jnp.where(sub == ix, before, 0.0), axis=0, keepdims=True).astype(I32)
             for ix in idxs]
    carry[...] = carry[...] + jnp.sum(onehot, axis=1, keepdims=True)
    cnt_ref[...] = carry[...]

    sub8 = lax.broadcasted_iota(I32, (2 * TOPK, tm), 0)
    meta = jnp.zeros((2 * TOPK, tm), I32)
    subw = lax.broadcasted_iota(I32, (LANES, tm), 0)
    wt = jnp.zeros((LANES, tm), F32)
    for k in range(TOPK):
        meta = jnp.where(sub8 == k, idxs[k], meta)
        meta = jnp.where(sub8 == TOPK + k, ranks[k], meta)
        wt = jnp.where(subw == k, ws[k], wt)
    meta_ref[...] = meta
    wts_ref[...] = jnp.transpose(wt)


def _mixer(geom, tm, xs, gb, p, sa, sf, f_p, f_s, mod3, conv_w, conv_b, wco, wfo, wo, g2,
           rw2, rwh, rb, utri):
    tt = geom.tt
    halo = 16
    hb = tm // halo
    nhalo = tt // halo
    row = lambda i: (i, 0)
    full = lambda i: (0, 0)
    x_specs = geom.split_specs(tm, D) if len(xs) == 2 else [pl.BlockSpec((tm, D), row)]
    body = functools.partial(_mixer_body, n_x=len(xs), npt=geom.tp // tm, tm=tm, geom=geom, halo=halo)
    return pl.pallas_call(
        body,
        grid=(tt // tm,),
        in_specs=x_specs + [
            pl.BlockSpec((tm, D), row),
            pl.BlockSpec((tm, D), row),
            pl.BlockSpec((halo, D), lambda i: (jnp.maximum(i * hb - 1, 0), 0)),
            pl.BlockSpec((halo, D), lambda i: (jnp.minimum((i + 1) * hb, nhalo - 1), 0)),
            pl.BlockSpec((tm, D), row),
            pl.BlockSpec((tm, D), row)] + geom.split_specs(tm, DF) + [
            pl.BlockSpec((None, 1, N_MOD * D), lambda i: (geom.seq_of(i * tm), 0, 0)),
            pl.BlockSpec((3, D), full),
            pl.BlockSpec((1, D), full),
            pl.BlockSpec((D, D), full),
            pl.BlockSpec((DF, D), full),
            pl.BlockSpec((D, D), full),
            pl.BlockSpec((1, D), full),
            pl.BlockSpec((D, 2 * LANES), full),
            pl.BlockSpec((D, LANES), full),
            pl.BlockSpec((1, LANES), full),
            pl.BlockSpec((tm, tm), full)],
        out_specs=[pl.BlockSpec((tm, D), row),
                   pl.BlockSpec((tm * ROW_TILE, LANES), row),
                   pl.BlockSpec((2 * TOPK, tm), lambda i: (0, i)),
                   pl.BlockSpec((tm, LANES), row),
                   pl.BlockSpec((E, LANES), full)],
        out_shape=[jax.ShapeDtypeStruct((tt, D), F32),
                   jax.ShapeDtypeStruct((tt * ROW_TILE, LANES), F32),
                   jax.ShapeDtypeStruct((2 * TOPK, tt), I32),
                   jax.ShapeDtypeStruct((tt, LANES), F32),
                   jax.ShapeDtypeStruct((E, LANES), F32)],
        scratch_shapes=[pltpu.VMEM((E, LANES), F32)],
        compiler_params=_params(("arbitrary",)),
        name="mixer_router",
    )(*xs, gb, p, p, p, sa, sf, f_p, f_s, mod3, conv_w, conv_b, wco, wfo, wo, g2, rw2, rwh, rb, utri)


def _row_tile(ref, r):
    return ref.at[pl.ds(pl.multiple_of(r * ROW_TILE, ROW_TILE), ROW_TILE), :]


def _dispatch_body(dest_ref, h_ref, xs_hbm, sem, *, tq):
    def issue(g, c):
        for u in range(DMA_GROUP):
            t = g * DMA_GROUP + u
            for k in range(TOPK):
                pltpu.make_async_copy(_row_tile(h_ref, t), _row_tile(xs_hbm, dest_ref[k, t]),
                                      sem).start(priority=(u * TOPK + k) % 2)
        return c

    lax.fori_loop(0, tq // DMA_GROUP, issue, 0)
    n = tq * TOPK * ROW_TILE
    pltpu.make_async_copy(xs_hbm.at[pl.ds(0, n), :], xs_hbm.at[pl.ds(0, n), :], sem).wait()


def _dispatch(tq, dest, h2r, n_rows):
    tt = h2r.shape[0] // ROW_TILE
    return pl.pallas_call(
        functools.partial(_dispatch_body, tq=tq),
        grid=(tt // tq,),
        in_specs=[pl.BlockSpec((TOPK, tq), lambda i: (0, i), memory_space=pltpu.SMEM),
                  pl.BlockSpec((tq * ROW_TILE, LANES), lambda i: (i, 0))],
        out_specs=pl.BlockSpec(memory_space=pl.ANY),
        out_shape=jax.ShapeDtypeStruct((n_rows * ROW_TILE, LANES), F32),
        scratch_shapes=[pltpu.SemaphoreType.DMA],
        compiler_params=_params(("arbitrary",)),
        name="moe_dispatch",
    )(dest, h2r)


def _ffn_body(be_ref, nv_ref, nu_ref, xs_ref, wgu_ref, bgu_ref, wdn_ref, bdn_ref, os_ref,
              wgu_bf, wdn_bf, *, bm):
    i = pl.program_id(0)

    @pl.when(i < nu_ref[0])
    def _():
        @pl.when((i == 0) | (be_ref[i] != be_ref[jnp.maximum(i - 1, 0)]))
        def _():
            wgu_bf[...] = wgu_ref[...].astype(BF16)
            wdn_bf[...] = wdn_ref[...].astype(BF16)

        x = _from_row_tiles(xs_ref, 0, bm)
        valid = lax.broadcasted_iota(I32, (bm, 1), 0) < nv_ref[i]
        x = jnp.where(valid, x, 0.0).astype(BF16)
        gu = jnp.dot(x, wgu_bf[...], preferred_element_type=F32) + bgu_ref[...]
        glu = jnp.minimum(gu[:, :DFF], LIMIT)
        lin = jnp.clip(gu[:, DFF:], -LIMIT, LIMIT)
        act = glu * jax.nn.sigmoid(ALPHA * glu) * (lin + 1.0)
        out = jnp.dot(act.astype(BF16), wdn_bf[...], preferred_element_type=F32) + bdn_ref[...]
        _to_row_tiles(os_ref, out, bm)


def _expert_ffn(bm, layer, block_expert, n_valid, n_used, xs, wgu, bgu, wdn, bdn):
    n_blocks = xs.shape[0] // (bm * ROW_TILE)
    rows = lambda i, be, nv, nu: (jnp.minimum(i, nu[0] - 1), 0)
    wsel = lambda i, be, nv, nu: (layer, be[i], 0, 0)
    grid_spec = pltpu.PrefetchScalarGridSpec(
        num_scalar_prefetch=3,
        grid=(n_blocks,),
        in_specs=[pl.BlockSpec((bm * ROW_TILE, LANES), rows),
                  pl.BlockSpec((None, None, D, 2 * DFF), wsel),
                  pl.BlockSpec((None, None, 1, 2 * DFF), wsel),
                  pl.BlockSpec((None, None, DFF, D), wsel),
                  pl.BlockSpec((None, None, 1, D), wsel)],
        out_specs=pl.BlockSpec((bm * ROW_TILE, LANES), rows),
        scratch_shapes=[pltpu.VMEM((D, 2 * DFF), BF16), pltpu.VMEM((DFF, D), BF16)],
    )
    return pl.pallas_call(
        functools.partial(_ffn_body, bm=bm),
        grid_spec=grid_spec,
        out_shape=jax.ShapeDtypeStruct(xs.shape, F32),
        compiler_params=_params(("arbitrary",)),
        name="expert_ffn",
    )(block_expert, n_valid, n_used, xs, wgu, bgu, wdn, bdn)


def _combine_body(dcur_ref, dnxt_ref, os_hbm, w_ref, x1_ref, mod_ref, g_ref, *rest, tq, final, npt):
    out_refs, (stage, sems) = rest[:-2], rest[-2:]
    i = pl.program_id(0)
    n = pl.num_programs(0)

    def issue(dref, slot):
        def grp(g, c):
            for u in range(DMA_GROUP):
                t = g * DMA_GROUP + u
                for k in range(TOPK):
                    dst = stage.at[slot, pl.ds(pl.multiple_of((k * tq + t) * ROW_TILE, ROW_TILE), ROW_TILE), :]
                    pltpu.make_async_copy(_row_tile(os_hbm, dref[k, t]), dst,
                                          sems.at[slot]).start(priority=(u * TOPK + k) % 2)
            return c
        lax.fori_loop(0, tq // DMA_GROUP, grp, 0)

    @pl.when(i == 0)
    def _():
        issue(dcur_ref, 0)

    @pl.when(i + 1 < n)
    def _():
        issue(dnxt_ref, (i + 1) % 2)

    slot = i % 2
    nrow = TOPK * tq * ROW_TILE
    pltpu.make_async_copy(os_hbm.at[pl.ds(0, nrow), :], stage.at[slot], sems.at[slot]).wait()
    buf = stage.at[slot]
    y = w_ref[:, 0:1] * _from_row_tiles(buf, 0, tq)
    for k in range(1, TOPK):
        y = y + w_ref[:, k:k + 1] * _from_row_tiles(buf, k * tq * ROW_TILE, tq)
    x2 = x1_ref[...] + mod_ref[:, 5 * D:6 * D] * y
    if not final:
        out_refs[0][...] = x2
    else:
        ms = jnp.mean(x2 * x2, axis=-1, keepdims=True)
        x2 = x2 * lax.rsqrt(ms + EPS) * g_ref[...]

        @pl.when(i < npt)
        def _():
            out_refs[0][...] = x2

        @pl.when(i >= npt)
        def _():
            out_refs[1][...] = x2


def _combine(geom, tq, dest, os_rows, wts, x1, mod3, g_final, final):
    tt = geom.tt
    nt = tt // tq
    if final:
        out_specs = geom.split_specs(tq, D)
        out_shape = [jax.ShapeDtypeStruct((geom.tp, D), F32), jax.ShapeDtypeStruct((geom.ts, D), F32)]
    else:
        out_specs = [pl.BlockSpec((tq, D), lambda i: (i, 0))]
        out_shape = [jax.ShapeDtypeStruct((tt, D), F32)]
    return pl.pallas_call(
        functools.partial(_combine_body, tq=tq, final=final, npt=geom.tp // tq),
        grid=(nt,),
        in_specs=[pl.BlockSpec((TOPK, tq), lambda i: (0, i), memory_space=pltpu.SMEM),
                  pl.BlockSpec((TOPK, tq), lambda i: (0, jnp.minimum(i + 1, nt - 1)),
                               memory_space=pltpu.SMEM),
                  pl.BlockSpec(memory_space=pl.ANY),
                  pl.BlockSpec((tq, LANES), lambda i: (i, 0)),
                  pl.BlockSpec((tq, D), lambda i: (i, 0)),
                  pl.BlockSpec((None, 1, N_MOD * D), lambda i: (geom.seq_of(i * tq), 0, 0)),
                  pl.BlockSpec((1, D), lambda i: (0, 0))],
        out_specs=out_specs,
        out_shape=out_shape,
        scratch_shapes=[pltpu.VMEM((2, TOPK * tq * ROW_TILE, LANES), F32),
                        pltpu.SemaphoreType.DMA((2,))],
        compiler_params=_params(("arbitrary",)),
        name="moe_combine",
    )(dest, dest, os_rows, wts, x1, mod3, g_final)


def _route_plan(meta_t, counts_f, bm, n_blocks):
    idx = meta_t[0:TOPK]
    rank = meta_t[TOPK:2 * TOPK]
    counts = counts_f[:, 0].astype(I32)
    padded = (counts + bm - 1) // bm * bm
    pad_end = jnp.cumsum(padded)
    pad_start = pad_end - padded
    dest = rank
    for e in range(E):
        dest = dest + jnp.where(idx == e, pad_start[e], 0)
    n_used = (pad_end[E - 1] // bm).astype(I32)
    blk = jnp.arange(n_blocks, dtype=I32)
    be = jnp.minimum(jnp.sum((pad_end[None, :] <= (blk * bm)[:, None]).astype(I32), axis=1), E - 1)
    be = jnp.where(blk < n_used, be, be[n_used - 1]).astype(I32)
    n_valid = jnp.clip((pad_start + counts)[be] - blk * bm, 0, bm).astype(I32)
    return dest, be, n_valid, n_used.reshape(1)


def _trunk(geom, x_p, x_s, c_all, mod_w, mod_b, norm1_g, norm2_g, w_in, conv_w, conv_b, w_conv_out,
           w_fourier_out, w_o, router_w, router_b, w_gate_up, b_gate_up, w_down, b_down, final_g,
           tm=512, tq=256, bm=512):
    depth = mod_w.shape[0]
    tt = geom.tt
    n_blocks = -(-tt * TOPK // bm) + E
    n_rows = n_blocks * bm
    cat = jnp.asarray(_chan_dft(), BF16)
    utri = jnp.asarray(np.triu(np.ones((tm, tm), np.float32), 1), BF16)
    c_pad = jnp.zeros((-(-geom.nseq // 8) * 8, D), F32).at[:geom.nseq].set(c_all)
    xs = (x_p, x_s)
    for l in range(depth):
        mod = _modulation(c_pad, mod_w[l], mod_b[l])
        mod3 = mod.reshape(mod.shape[0], 1, N_MOD * D)
        gb, p, sa, sf, a_p, b_p, a_s, b_s = _inproj(geom, tm, xs, mod3, norm1_g[l].reshape(1, D),
                                                    w_in[l].astype(BF16), cat)
        f_p = _fourier(a_p, b_p, geom.bp, geom.sp)
        f_s = _fourier(a_s, b_s, geom.bs, geom.ss)
        rw = jnp.zeros((D, LANES), F32).at[:, :E].set(router_w[l])
        rwh = rw.astype(BF16)
        rw2 = jnp.concatenate([rwh, (rw - rwh.astype(F32)).astype(BF16)], axis=1)
        rb = jnp.full((1, LANES), NEG, F32).at[0, :E].set(router_b[l])
        x1, h2r, meta_t, wts, counts = _mixer(
            geom, tm, xs, gb, p, sa, sf, f_p, f_s, mod3, conv_w[l], conv_b[l].reshape(1, D),
            w_conv_out[l].astype(BF16), w_fourier_out[l].astype(BF16), w_o[l].astype(BF16),
            norm2_g[l].reshape(1, D), rw2, rwh, rb, utri)
        dest, block_expert, n_valid, n_used = _route_plan(meta_t, counts, bm, n_blocks)
        xrows = _dispatch(tq, dest, h2r, n_rows)
        os_rows = _expert_ffn(bm, l, block_expert, n_valid, n_used, xrows,
                              w_gate_up, b_gate_up.reshape(depth, E, 1, 2 * DFF),
                              w_down, b_down.reshape(depth, E, 1, D))
        xs = tuple(_combine(geom, tq, dest, os_rows, wts, x1, mod3, final_g.reshape(1, D),
                            final=(l == depth - 1)))
    return xs


def kernel(x_prompt, x_sample, c_prompt, c_sample, mod_w, mod_b, norm1_g, norm2_g, w_in, conv_w,
           conv_b, w_conv_out, w_fourier_out, w_o, router_w, router_b, w_gate_up, b_gate_up,
           w_down, b_down, final_g):
    bp, sp, _ = x_prompt.shape
    bs, ss, _ = x_sample.shape
    geom = _Geom(bp, sp, bs, ss)
    c_all = jnp.concatenate([c_prompt, c_sample], axis=0)
    y_p, y_s = _trunk(geom, x_prompt.reshape(bp * sp, D), x_sample.reshape(bs * ss, D), c_all,
                      mod_w, mod_b, norm1_g, norm2_g, w_in, conv_w, conv_b, w_conv_out,
                      w_fourier_out, w_o, router_w, router_b, w_gate_up, b_gate_up, w_down, b_down,
                      final_g)
    return (y_p.reshape(bp, sp, D), y_s.reshape(bs, ss, D))
```

```python
import functools
import math

import numpy as np
import jax
import jax.numpy as jnp
from jax import lax
from jax.experimental import pallas as pl
from jax.experimental.pallas import tpu as pltpu

F32 = jnp.float32
BF16 = jnp.bfloat16
I32 = jnp.int32

D = 1024
DF = 512
NG = 4
DG = DF // NG
E = 32
TOPK = 4
DFF = 1024
N_MOD = 6
ALPHA = 1.702
LIMIT = 7.0
EPS = 1e-5
LANES = 128
SUBLANES = 8
ROW_TILE = D // LANES
FFT_N1 = 128
FFT_K1_TILE = SUBLANES
VMEM_LIMIT = 56 * 1024 * 1024
NEG = -1e30
DMA_GROUP = 8

assert ROW_TILE == SUBLANES


def _params(sem, vmem=VMEM_LIMIT):
    return pltpu.CompilerParams(dimension_semantics=sem, vmem_limit_bytes=vmem)


class _Geom:
    def __init__(self, bp, sp, bs, ss):
        assert bp > 0 and bs > 0
        self.bp, self.sp, self.bs, self.ss = bp, sp, bs, ss
        self.tp = bp * sp
        self.ts = bs * ss
        self.tt = self.tp + self.ts
        self.nseq = bp + bs

    def seq_of(self, t0):
        return jnp.where(t0 < self.tp, t0 // self.sp, self.bp + (t0 - self.tp) // self.ss)

    def offset_len(self, t0):
        in_p = t0 < self.tp
        off = jnp.where(in_p, t0 % self.sp, (t0 - self.tp) % self.ss)
        slen = jnp.where(in_p, self.sp, self.ss)
        return off, slen

    def split_specs(self, tm, width):
        npt = self.tp // tm
        return [pl.BlockSpec((tm, width), lambda i: (jnp.minimum(i, npt - 1), 0)),
                pl.BlockSpec((tm, width), lambda i: (jnp.maximum(i - npt, 0), 0))]


def _to_row_tiles(ref, val, rows):
    for s in range(ROW_TILE):
        ref[pl.ds(s, rows, stride=ROW_TILE), :] = val[:, s * LANES:(s + 1) * LANES]


def _from_row_tiles(ref, start, rows):
    return jnp.concatenate(
        [ref[pl.ds(start + s, rows, stride=ROW_TILE), :] for s in range(ROW_TILE)], axis=1)


def _mod_body(c_ref, w_ref, b_ref, o_ref):
    c = c_ref[...]
    s = c * jax.nn.sigmoid(c)
    o_ref[...] = jnp.dot(s, w_ref[...], preferred_element_type=F32,
                         precision=lax.Precision.HIGHEST) + b_ref[...]


def _modulation(c_pad, mod_w, mod_b):
    rows = c_pad.shape[0]
    return pl.pallas_call(
        _mod_body,
        grid=(N_MOD,),
        in_specs=[pl.BlockSpec((rows, D), lambda j: (0, 0)),
                  pl.BlockSpec((D, D), lambda j: (0, j)),
                  pl.BlockSpec((1, D), lambda j: (0, j))],
        out_specs=pl.BlockSpec((rows, D), lambda j: (0, j)),
        out_shape=jax.ShapeDtypeStruct((rows, N_MOD * D), F32),
        compiler_params=_params(("parallel",)),
        name="modulation",
    )(c_pad, mod_w, mod_b.reshape(1, N_MOD * D))


def _inproj_body(*refs, n_x, npt):
    x_refs = refs[:n_x]
    (mod_ref, g_ref, w_ref, cat_ref,
     gb_ref, p_ref, sa_ref, sf_ref, ap_ref, bp_ref, as_ref, bs_ref) = refs[n_x:]
    is_p = pl.program_id(0) < npt
    x = x_refs[0][...] if n_x == 1 else jnp.where(is_p, x_refs[0][...], x_refs[1][...])
    ms = jnp.mean(x * x, axis=-1, keepdims=True)
    xn = x * lax.rsqrt(ms + EPS) * g_ref[...]
    shift = mod_ref[:, 0:D]
    scale = mod_ref[:, D:2 * D]
    h = (xn * (1.0 + scale) + shift).astype(BF16)

    def proj(lo, hi):
        return jnp.dot(h, w_ref[:, lo:hi], preferred_element_type=F32)

    gb_ref[...] = proj(0, D).astype(BF16)
    p_ref[...] = (proj(D, 2 * D) * proj(2 * D, 3 * D)).astype(BF16)
    uf = proj(3 * D, 3 * D + DF).astype(BF16)
    abs_ = [jnp.dot(uf[:, g * DG:(g + 1) * DG], cat_ref[...], preferred_element_type=F32)
            for g in range(NG)]

    def put_ab(a_ref, b_ref):
        for g in range(NG):
            a_ref[g] = abs_[g][:, :DG]
            b_ref[g] = abs_[g][:, DG:]

    @pl.when(is_p)
    def _():
        put_ab(ap_ref, bp_ref)

    @pl.when(jnp.logical_not(is_p))
    def _():
        put_ab(as_ref, bs_ref)

    sa_ref[...] = jax.nn.sigmoid(proj(3 * D + DF, 4 * D + DF)).astype(BF16)
    sf_ref[...] = jax.nn.sigmoid(proj(4 * D + DF, 5 * D + DF)).astype(BF16)


def _inproj(geom, tm, xs, mod3, g1, w_in_bf, cat):
    tt = geom.tt
    npt = geom.tp // tm
    row = lambda i: (i, 0)
    full = lambda i: (0, 0)
    bf = lambda r, n: jax.ShapeDtypeStruct((r, n), BF16)
    x_specs = geom.split_specs(tm, D) if len(xs) == 2 else [pl.BlockSpec((tm, D), row)]
    ab_specs = [pl.BlockSpec((NG, tm, DG), lambda i: (0, jnp.minimum(i, npt - 1), 0)),
                pl.BlockSpec((NG, tm, DG), lambda i: (0, jnp.maximum(i - npt, 0), 0))]
    ab_shape = lambda r: jax.ShapeDtypeStruct((NG, r, DG), F32)
    return pl.pallas_call(
        functools.partial(_inproj_body, n_x=len(xs), npt=npt),
        grid=(tt // tm,),
        in_specs=x_specs + [
            pl.BlockSpec((None, 1, N_MOD * D), lambda i: (geom.seq_of(i * tm), 0, 0)),
            pl.BlockSpec((1, D), full),
            pl.BlockSpec(w_in_bf.shape, full),
            pl.BlockSpec(cat.shape, full)],
        out_specs=[pl.BlockSpec((tm, D), row), pl.BlockSpec((tm, D), row),
                   pl.BlockSpec((tm, D), row), pl.BlockSpec((tm, D), row),
                   ab_specs[0], ab_specs[0], ab_specs[1], ab_specs[1]],
        out_shape=[bf(tt, D), bf(tt, D), bf(tt, D), bf(tt, D),
                   ab_shape(geom.tp), ab_shape(geom.tp), ab_shape(geom.ts), ab_shape(geom.ts)],
        compiler_params=_params(("arbitrary",)),
        name="inproj",
    )(*xs, mod3, g1, w_in_bf, cat)


@functools.lru_cache(maxsize=None)
def _fft_consts(s):
    n1 = FFT_N1
    n2 = s // n1
    a = np.arange(n1)
    ang = 2.0 * np.pi * (np.outer(a, a) % n1) / n1
    c1, s1 = np.cos(ang), np.sin(ang)
    w1 = np.block([[c1, s1], [-s1, c1]])
    ang = 2.0 * np.pi * (np.outer(np.arange(n1), np.arange(n2)) % s) / s
    tc = np.repeat(np.cos(ang), LANES, axis=1)
    ts = np.repeat(np.sin(ang), LANES, axis=1)
    m = np.arange(n2)
    ang = 2.0 * np.pi * (np.outer(m, m) % n2) / n2
    c2, s2 = np.cos(ang), np.sin(ang)
    kt = FFT_K1_TILE
    wexp = np.zeros((n2, kt, 2, kt, n2))
    for j in range(kt):
        wexp[:, j, 0, j, :] = c2
        wexp[:, j, 1, j, :] = s2
    wexp = wexp.reshape(n2 * kt, 2 * kt * n2)
    return (np.asarray(w1, np.float32), np.asarray(tc, np.float32), np.asarray(ts, np.float32),
            np.asarray(wexp, np.float32))


@functools.lru_cache(maxsize=None)
def _chan_dft():
    a = np.arange(DG)
    ang = 2.0 * np.pi * (np.outer(a, a) % DG) / DG
    return np.asarray(np.concatenate([np.cos(ang), -np.sin(ang)], axis=1), np.float32)


def _fft1_body(a_ref, b_ref, w_ref, tc_ref, ts_ref, yr_ref, yi_ref, abuf, bbuf, yrbuf, yibuf, *, tn2):
    n1 = FFT_N1
    for g in range(NG):
        abuf[g] = a_ref[g].reshape(n1 * tn2, LANES)
        bbuf[g] = b_ref[g].reshape(n1 * tn2, LANES)

    for j in range(tn2):
        sl = pl.ds(j, n1, stride=tn2)
        a = jnp.concatenate([abuf[g, sl, :] for g in range(NG)], axis=1)
        b = jnp.concatenate([bbuf[g, sl, :] for g in range(NG)], axis=1)
        ab = jnp.concatenate([a, b], axis=0).astype(BF16)
        y = jnp.dot(w_ref[...], ab, preferred_element_type=F32)
        c = tc_ref[:, j * LANES:(j + 1) * LANES]
        s = ts_ref[:, j * LANES:(j + 1) * LANES]
        for g in range(NG):
            yr = y[:n1, g * LANES:(g + 1) * LANES]
            yi = y[n1:, g * LANES:(g + 1) * LANES]
            yrbuf[g, sl, :] = yr * c + yi * s
            yibuf[g, sl, :] = yi * c - yr * s

    for g in range(NG):
        yr_ref[g] = yrbuf[g].reshape(n1, tn2, LANES)
        yi_ref[g] = yibuf[g].reshape(n1, tn2, LANES)


def _fft2_body(yr_ref, yi_ref, w_ref, f_ref, *, n2, scale):
    yr = jnp.concatenate([yr_ref[g] for g in range(NG)], axis=1)
    yi = jnp.concatenate([yi_ref[g] for g in range(NG)], axis=1)
    rhs = jnp.concatenate([yr, yi], axis=0).astype(BF16)
    z = jnp.dot(w_ref[...], rhs, preferred_element_type=F32) * scale
    f_ref[...] = z.reshape(n2, FFT_K1_TILE, DF)


def _fourier(a, b, nseq, s):
    n1 = FFT_N1
    n2 = s // n1
    tn2 = SUBLANES
    kt = FFT_K1_TILE
    w1, tc, ts, wexp = _fft_consts(s)
    a5 = a.reshape(NG, nseq, n1, n2, DG)
    b5 = b.reshape(NG, nseq, n1, n2, DG)
    blk = pl.BlockSpec((NG, None, n1, tn2, DG), lambda q, j: (0, q, 0, j, 0))
    yr, yi = pl.pallas_call(
        functools.partial(_fft1_body, tn2=tn2),
        grid=(nseq, n2 // tn2),
        in_specs=[blk, blk,
                  pl.BlockSpec((2 * n1, 2 * n1), lambda q, j: (0, 0)),
                  pl.BlockSpec((n1, tn2 * LANES), lambda q, j: (0, j)),
                  pl.BlockSpec((n1, tn2 * LANES), lambda q, j: (0, j))],
        out_specs=[blk, blk],
        out_shape=[jax.ShapeDtypeStruct((NG, nseq, n1, n2, DG), F32)] * 2,
        scratch_shapes=[pltpu.VMEM((NG, n1 * tn2, DG), F32)] * 4,
        compiler_params=_params(("parallel", "parallel")),
        name="fft_stage1",
    )(a5, b5, jnp.asarray(w1, BF16), jnp.asarray(tc), jnp.asarray(ts))
    yr = yr.reshape(NG, nseq, n1 * n2, DG)
    yi = yi.reshape(NG, nseq, n1 * n2, DG)
    yblk = pl.BlockSpec((NG, None, kt * n2, DG), lambda q, j: (0, q, j, 0))
    f = pl.pallas_call(
        functools.partial(_fft2_body, n2=n2, scale=1.0 / math.sqrt(s * DG)),
        grid=(nseq, n1 // kt),
        in_specs=[yblk, yblk,
                  pl.BlockSpec(wexp.shape, lambda q, j: (0, 0))],
        out_specs=pl.BlockSpec((None, n2, kt, DF), lambda q, j: (q, 0, j, 0)),
        out_shape=jax.ShapeDtypeStruct((nseq, n2, n1, DF), F32),
        compiler_params=_params(("parallel", "parallel")),
        name="fft_stage2",
    )(yr, yi, jnp.asarray(wexp, BF16))
    return f.reshape(nseq * s, DF)


def _mixer_body(*refs, n_x, npt, tm, geom, halo):
    x_refs = refs[:n_x]
    (gb_ref, p_ref, pprev_ref, pnext_ref, sa_ref, sf_ref, fp_ref, fs_ref, mod_ref,
     cw_ref, cb_ref, wco_ref, wfo_ref, wo_ref, g2_ref, rw2_ref, rwh_ref, rb_ref, utri_ref,
     x1_ref, h2_ref, meta_ref, wts_ref, cnt_ref, carry) = refs[n_x:]
    i = pl.program_id(0)
    is_p = i < npt

    @pl.when(i == 0)
    def _():
        carry[...] = jnp.zeros_like(carry)

    x = x_refs[0][...] if n_x == 1 else jnp.where(is_p, x_refs[0][...], x_refs[1][...])
    f = jnp.where(is_p, fp_ref[...], fs_ref[...])

    off, slen = geom.offset_len(i * tm)
    keep_prev = jnp.where(off == 0, 0.0, 1.0)
    keep_next = jnp.where(off + tm == slen, 0.0, 1.0)
    p = p_ref[...].astype(F32)
    prev_row = pprev_ref[halo - 1:halo, :].astype(F32) * keep_prev
    next_row = pnext_ref[0:1, :].astype(F32) * keep_next
    row = lax.broadcasted_iota(I32, (tm, 1), 0)
    pm = jnp.where(row == 0, prev_row, pltpu.roll(p, 1, 0))
    pp = jnp.where(row == tm - 1, next_row, pltpu.roll(p, tm - 1, 0))
    conv = cw_ref[0:1, :] * pm + cw_ref[1:2, :] * p + cw_ref[2:3, :] * pp + cb_ref[...]
    ya_in = (gb_ref[...].astype(F32) * conv).astype(BF16)
    ya = jnp.dot(ya_in, wco_ref[...], preferred_element_type=F32)
    yf = jnp.dot(f.astype(BF16), wfo_ref[...], preferred_element_type=F32)
    merged = sa_ref[...].astype(F32) * ya + sf_ref[...].astype(F32) * yf
    mix = jnp.dot(merged.astype(BF16), wo_ref[...], preferred_element_type=F32)
    x1 = x + mod_ref[:, 2 * D:3 * D] * mix
    x1_ref[...] = x1

    ms = jnp.mean(x1 * x1, axis=-1, keepdims=True)
    xn = x1 * lax.rsqrt(ms + EPS) * g2_ref[...]
    h2 = xn * (1.0 + mod_ref[:, 4 * D:5 * D]) + mod_ref[:, 3 * D:4 * D]
    _to_row_tiles(h2_ref, h2, tm)

    hh = h2.astype(BF16)
    hl = (h2 - hh.astype(F32)).astype(BF16)
    two = jnp.dot(hh, rw2_ref[...], preferred_element_type=F32)
    logits = (two[:, :LANES] + two[:, LANES:]
              + jnp.dot(hl, rwh_ref[...], preferred_element_type=F32)) + rb_ref[...]

    l = jnp.transpose(logits)[0:E, :]
    sub = lax.broadcasted_iota(I32, (E, tm), 0)
    vals, idxs = [], []
    for _ in range(TOPK):
        m = jnp.max(l, axis=0, keepdims=True)
        ix = jnp.min(jnp.where(l == m, sub, E), axis=0, keepdims=True)
        vals.append(m)
        idxs.append(ix)
        l = jnp.where(sub == ix, -jnp.inf, l)
    exps = [jnp.exp(v - vals[0]) for v in vals]
    denom = exps[0] + exps[1] + exps[2] + exps[3]
    ws = [e / denom for e in exps]

    onehot = jnp.zeros((E, tm), F32)
    for ix in idxs:
        onehot = onehot + jnp.where(sub == ix, 1.0, 0.0)
    before = jnp.dot(onehot.astype(BF16), utri_ref[...], preferred_element_type=F32) + carry[:, 0:1]
    ranks = [jnp.sum(jnp.where(sub == ix, before, 0.0), axis=0, keepdims=True).astype(I32)
             for ix in idxs]
    carry[...] = carry[...] + jnp.sum(onehot, axis=1, keepdims=True)
    cnt_ref[...] = carry[...]

    sub8 = lax.broadcasted_iota(I32, (2 * TOPK, tm), 0)
    meta = jnp.zeros((2 * TOPK, tm), I32)
    subw = lax.broadcasted_iota(I32, (LANES, tm), 0)
    wt = jnp.zeros((LANES, tm), F32)
    for k in range(TOPK):
        meta = jnp.where(sub8 == k, idxs[k], meta)
        meta = jnp.where(sub8 == TOPK + k, ranks[k], meta)
        wt = jnp.where(subw == k, ws[k], wt)
    meta_ref[...] = meta
    wts_ref[...] = jnp.transpose(wt)


def _mixer(geom, tm, xs, gb, p, sa, sf, f_p, f_s, mod3, conv_w, conv_b, wco, wfo, wo, g2,
           rw2, rwh, rb, utri):
    tt = geom.tt
    halo = 16
    hb = tm // halo
    nhalo = tt // halo
    row = lambda i: (i, 0)
    full = lambda i: (0, 0)
    x_specs = geom.split_specs(tm, D) if len(xs) == 2 else [pl.BlockSpec((tm, D), row)]
    body = functools.partial(_mixer_body, n_x=len(xs), npt=geom.tp // tm, tm=tm, geom=geom, halo=halo)
    return pl.pallas_call(
        body,
        grid=(tt // tm,),
        in_specs=x_specs + [
            pl.BlockSpec((tm, D), row),
            pl.BlockSpec((tm, D), row),
            pl.BlockSpec((halo, D), lambda i: (jnp.maximum(i * hb - 1, 0), 0)),
            pl.BlockSpec((halo, D), lambda i: (jnp.minimum((i + 1) * hb, nhalo - 1), 0)),
            pl.BlockSpec((tm, D), row),
            pl.BlockSpec((tm, D), row)] + geom.split_specs(tm, DF) + [
            pl.BlockSpec((None, 1, N_MOD * D), lambda i: (geom.seq_of(i * tm), 0, 0)),
            pl.BlockSpec((3, D), full),
            pl.BlockSpec((1, D), full),
            pl.BlockSpec((D, D), full),
            pl.BlockSpec((DF, D), full),
            pl.BlockSpec((D, D), full),
            pl.BlockSpec((1, D), full),
            pl.BlockSpec((D, 2 * LANES), full),
            pl.BlockSpec((D, LANES), full),
            pl.BlockSpec((1, LANES), full),
            pl.BlockSpec((tm, tm), full)],
        out_specs=[pl.BlockSpec((tm, D), row),
                   pl.BlockSpec((tm * ROW_TILE, LANES), row),
                   pl.BlockSpec((2 * TOPK, tm), lambda i: (0, i)),
                   pl.BlockSpec((tm, LANES), row),
                   pl.BlockSpec((E, LANES), full)],
        out_shape=[jax.ShapeDtypeStruct((tt, D), F32),
                   jax.ShapeDtypeStruct((tt * ROW_TILE, LANES), F32),
                   jax.ShapeDtypeStruct((2 * TOPK, tt), I32),
                   jax.ShapeDtypeStruct((tt, LANES), F32),
                   jax.ShapeDtypeStruct((E, LANES), F32)],
        scratch_shapes=[pltpu.VMEM((E, LANES), F32)],
        compiler_params=_params(("arbitrary",)),
        name="mixer_router",
    )(*xs, gb, p, p, p, sa, sf, f_p, f_s, mod3, conv_w, conv_b, wco, wfo, wo, g2, rw2, rwh, rb, utri)


def _row_tile(ref, r):
    return ref.at[pl.ds(pl.multiple_of(r * ROW_TILE, ROW_TILE), ROW_TILE), :]


def _dest_base(g):
    per_chunk = LANES // DMA_GROUP
    chunk = lax.shift_right_logical(g, per_chunk.bit_length() - 1)
    return chunk * (TOPK * LANES) + (g & (per_chunk - 1)) * DMA_GROUP


def _chunk_dest(dest):
    tt = dest.shape[1]
    return dest.reshape(TOPK, tt // LANES, LANES).transpose(1, 0, 2).reshape(-1)


def _dispatch_body(dest_ref, h_ref, xs_hbm, sem, *, tq):
    def issue(g, c):
        base = _dest_base(g)
        for u in range(DMA_GROUP):
            t = g * DMA_GROUP + u
            for k in range(TOPK):
                d = dest_ref[base + (k * LANES + u)]
                pltpu.make_async_copy(_row_tile(h_ref, t), _row_tile(xs_hbm, d),
                                      sem).start(priority=(u * TOPK + k) % 2)
        return c

    lax.fori_loop(0, tq // DMA_GROUP, issue, 0)
    n = tq * TOPK * ROW_TILE
    pltpu.make_async_copy(xs_hbm.at[pl.ds(0, n), :], xs_hbm.at[pl.ds(0, n), :], sem).wait()


def _dispatch(tq, dest, h2r, n_rows):
    tt = h2r.shape[0] // ROW_TILE
    return pl.pallas_call(
        functools.partial(_dispatch_body, tq=tq),
        grid=(tt // tq,),
        in_specs=[pl.BlockSpec((TOPK * tq,), lambda i: (i,), memory_space=pltpu.SMEM),
                  pl.BlockSpec((tq * ROW_TILE, LANES), lambda i: (i, 0))],
        out_specs=pl.BlockSpec(memory_space=pl.ANY),
        out_shape=jax.ShapeDtypeStruct((n_rows * ROW_TILE, LANES), F32),
        scratch_shapes=[pltpu.SemaphoreType.DMA],
        compiler_params=_params(("arbitrary",)),
        name="moe_dispatch",
    )(dest, h2r)


def _ffn_body(be_ref, nv_ref, nu_ref, xs_ref, wgu_ref, bgu_ref, wdn_ref, bdn_ref, os_ref,
              wgu_bf, wdn_bf, *, bm):
    i = pl.program_id(0)

    @pl.when(i < nu_ref[0])
    def _():
        @pl.when((i == 0) | (be_ref[i] != be_ref[jnp.maximum(i - 1, 0)]))
        def _():
            wgu_bf[...] = wgu_ref[...].astype(BF16)
            wdn_bf[...] = wdn_ref[...].astype(BF16)

        x = _from_row_tiles(xs_ref, 0, bm)
        valid = lax.broadcasted_iota(I32, (bm, 1), 0) < nv_ref[i]
        x = jnp.where(valid, x, 0.0).astype(BF16)
        gu = jnp.dot(x, wgu_bf[...], preferred_element_type=F32) + bgu_ref[...]
        glu = jnp.minimum(gu[:, :DFF], LIMIT)
        lin = jnp.clip(gu[:, DFF:], -LIMIT, LIMIT)
        act = glu * jax.nn.sigmoid(ALPHA * glu) * (lin + 1.0)
        out = jnp.dot(act.astype(BF16), wdn_bf[...], preferred_element_type=F32) + bdn_ref[...]
        _to_row_tiles(os_ref, out, bm)


def _expert_ffn(bm, layer, block_expert, n_valid, n_used, xs, wgu, bgu, wdn, bdn):
    n_blocks = xs.shape[0] // (bm * ROW_TILE)
    rows = lambda i, be, nv, nu: (jnp.minimum(i, nu[0] - 1), 0)
    wsel = lambda i, be, nv, nu: (layer, be[i], 0, 0)
    grid_spec = pltpu.PrefetchScalarGridSpec(
        num_scalar_prefetch=3,
        grid=(n_blocks,),
        in_specs=[pl.BlockSpec((bm * ROW_TILE, LANES), rows),
                  pl.BlockSpec((None, None, D, 2 * DFF), wsel),
                  pl.BlockSpec((None, None, 1, 2 * DFF), wsel),
                  pl.BlockSpec((None, None, DFF, D), wsel),
                  pl.BlockSpec((None, None, 1, D), wsel)],
        out_specs=pl.BlockSpec((bm * ROW_TILE, LANES), rows),
        scratch_shapes=[pltpu.VMEM((D, 2 * DFF), BF16), pltpu.VMEM((DFF, D), BF16)],
    )
    return pl.pallas_call(
        functools.partial(_ffn_body, bm=bm),
        grid_spec=grid_spec,
        out_shape=jax.ShapeDtypeStruct(xs.shape, F32),
        compiler_params=_params(("arbitrary",)),
        name="expert_ffn",
    )(block_expert, n_valid, n_used, xs, wgu, bgu, wdn, bdn)


def _combine_body(dcur_ref, dnxt_ref, os_hbm, w_ref, x1_ref, mod_ref, g_ref, *rest, tq, final, npt):
    out_refs, (stage, sems) = rest[:-2], rest[-2:]
    i = pl.program_id(0)
    n = pl.num_programs(0)

    def issue(dref, slot):
        def grp(g, c):
            base = _dest_base(g)
            for u in range(DMA_GROUP):
                t = g * DMA_GROUP + u
                for k in range(TOPK):
                    dst = stage.at[slot, pl.ds(pl.multiple_of((k * tq + t) * ROW_TILE, ROW_TILE), ROW_TILE), :]
                    pltpu.make_async_copy(_row_tile(os_hbm, dref[base + (k * LANES + u)]), dst,
                                          sems.at[slot]).start(priority=(u * TOPK + k) % 2)
            return c
        lax.fori_loop(0, tq // DMA_GROUP, grp, 0)

    @pl.when(i == 0)
    def _():
        issue(dcur_ref, 0)

    @pl.when(i + 1 < n)
    def _():
        issue(dnxt_ref, (i + 1) % 2)

    slot = i % 2
    nrow = TOPK * tq * ROW_TILE
    pltpu.make_async_copy(os_hbm.at[pl.ds(0, nrow), :], stage.at[slot], sems.at[slot]).wait()
    buf = stage.at[slot]
    y = w_ref[:, 0:1] * _from_row_tiles(buf, 0, tq)
    for k in range(1, TOPK):
        y = y + w_ref[:, k:k + 1] * _from_row_tiles(buf, k * tq * ROW_TILE, tq)
    x2 = x1_ref[...] + mod_ref[:, 5 * D:6 * D] * y
    if not final:
        out_refs[0][...] = x2
    else:
        ms = jnp.mean(x2 * x2, axis=-1, keepdims=True)
        x2 = x2 * lax.rsqrt(ms + EPS) * g_ref[...]

        @pl.when(i < npt)
        def _():
            out_refs[0][...] = x2

        @pl.when(i >= npt)
        def _():
            out_refs[1][...] = x2


def _combine(geom, tq, dest, os_rows, wts, x1, mod3, g_final, final):
    tt = geom.tt
    nt = tt // tq
    if final:
        out_specs = geom.split_specs(tq, D)
        out_shape = [jax.ShapeDtypeStruct((geom.tp, D), F32), jax.ShapeDtypeStruct((geom.ts, D), F32)]
    else:
        out_specs = [pl.BlockSpec((tq, D), lambda i: (i, 0))]
        out_shape = [jax.ShapeDtypeStruct((tt, D), F32)]
    return pl.pallas_call(
        functools.partial(_combine_body, tq=tq, final=final, npt=geom.tp // tq),
        grid=(nt,),
        in_specs=[pl.BlockSpec((TOPK * tq,), lambda i: (i,), memory_space=pltpu.SMEM),
                  pl.BlockSpec((TOPK * tq,), lambda i: (jnp.minimum(i + 1, nt - 1),),
                               memory_space=pltpu.SMEM),
                  pl.BlockSpec(memory_space=pl.ANY),
                  pl.BlockSpec((tq, LANES), lambda i: (i, 0)),
                  pl.BlockSpec((tq, D), lambda i: (i, 0)),
                  pl.BlockSpec((None, 1, N_MOD * D), lambda i: (geom.seq_of(i * tq), 0, 0)),
                  pl.BlockSpec((1, D), lambda i: (0, 0))],
        out_specs=out_specs,
        out_shape=out_shape,
        scratch_shapes=[pltpu.VMEM((2, TOPK * tq * ROW_TILE, LANES), F32),
                        pltpu.SemaphoreType.DMA((2,))],
        compiler_params=_params(("arbitrary",)),
        name="moe_combine",
    )(dest, dest, os_rows, wts, x1, mod3, g_final)


def _route_plan(meta_t, counts_f, bm, n_blocks):
    idx = meta_t[0:TOPK]
    rank = meta_t[TOPK:2 * TOPK]
    counts = counts_f[:, 0].astype(I32)
    padded = (counts + bm - 1) // bm * bm
    pad_end = jnp.cumsum(padded)
    pad_start = pad_end - padded
    dest = rank
    for e in range(E):
        dest = dest + jnp.where(idx == e, pad_start[e], 0)
    n_used = (pad_end[E - 1] // bm).astype(I32)
    blk = jnp.arange(n_blocks, dtype=I32)
    be = jnp.minimum(jnp.sum((pad_end[None, :] <= (blk * bm)[:, None]).astype(I32), axis=1), E - 1)
    be = jnp.where(blk < n_used, be, be[n_used - 1]).astype(I32)
    n_valid = jnp.clip((pad_start + counts)[be] - blk * bm, 0, bm).astype(I32)
    return _chunk_dest(dest), be, n_valid, n_used.reshape(1)


def _trunk(geom, x_p, x_s, c_all, mod_w, mod_b, norm1_g, norm2_g, w_in, conv_w, conv_b, w_conv_out,
           w_fourier_out, w_o, router_w, router_b, w_gate_up, b_gate_up, w_down, b_down, final_g,
           tm=512, tq_dispatch=1024, tq_combine=512, bm=512):
    depth = mod_w.shape[0]
    tt = geom.tt
    n_blocks = -(-tt * TOPK // bm) + E
    n_rows = n_blocks * bm
    cat = jnp.asarray(_chan_dft(), BF16)
    utri = jnp.asarray(np.triu(np.ones((tm, tm), np.float32), 1), BF16)
    c_pad = jnp.zeros((-(-geom.nseq // 8) * 8, D), F32).at[:geom.nseq].set(c_all)
    xs = (x_p, x_s)
    for l in range(depth):
        mod = _modulation(c_pad, mod_w[l], mod_b[l])
        mod3 = mod.reshape(mod.shape[0], 1, N_MOD * D)
        gb, p, sa, sf, a_p, b_p, a_s, b_s = _inproj(geom, tm, xs, mod3, norm1_g[l].reshape(1, D),
                                                    w_in[l].astype(BF16), cat)
        f_p = _fourier(a_p, b_p, geom.bp, geom.sp)
        f_s = _fourier(a_s, b_s, geom.bs, geom.ss)
        rw = jnp.zeros((D, LANES), F32).at[:, :E].set(router_w[l])
        rwh = rw.astype(BF16)
        rw2 = jnp.concatenate([rwh, (rw - rwh.astype(F32)).astype(BF16)], axis=1)
        rb = jnp.full((1, LANES), NEG, F32).at[0, :E].set(router_b[l])
        x1, h2r, meta_t, wts, counts = _mixer(
            geom, tm, xs, gb, p, sa, sf, f_p, f_s, mod3, conv_w[l], conv_b[l].reshape(1, D),
            w_conv_out[l].astype(BF16), w_fourier_out[l].astype(BF16), w_o[l].astype(BF16),
            norm2_g[l].reshape(1, D), rw2, rwh, rb, utri)
        dest, block_expert, n_valid, n_used = _route_plan(meta_t, counts, bm, n_blocks)
        xrows = _dispatch(tq_dispatch, dest, h2r, n_rows)
        os_rows = _expert_ffn(bm, l, block_expert, n_valid, n_used, xrows,
                              w_gate_up, b_gate_up.reshape(depth, E, 1, 2 * DFF),
                              w_down, b_down.reshape(depth, E, 1, D))
        xs = tuple(_combine(geom, tq_combine, dest, os_rows, wts, x1, mod3, final_g.reshape(1, D),
                            final=(l == depth - 1)))
    return xs


def kernel(x_prompt, x_sample, c_prompt, c_sample, mod_w, mod_b, norm1_g, norm2_g, w_in, conv_w,
           conv_b, w_conv_out, w_fourier_out, w_o, router_w, router_b, w_gate_up, b_gate_up,
           w_down, b_down, final_g):
    bp, sp, _ = x_prompt.shape
    bs, ss, _ = x_sample.shape
    geom = _Geom(bp, sp, bs, ss)
    c_all = jnp.concatenate([c_prompt, c_sample], axis=0)
    y_p, y_s = _trunk(geom, x_prompt.reshape(bp * sp, D), x_sample.reshape(bs * ss, D), c_all,
                      mod_w, mod_b, norm1_g, norm2_g, w_in, conv_w, conv_b, w_conv_out,
                      w_fourier_out, w_o, router_w, router_b, w_gate_up, b_gate_up, w_down, b_down,
                      final_g)
    return (y_p.reshape(bp, sp, D), y_s.reshape(bs, ss, D))
```

```python
import functools
import math

import numpy as np
import jax
import jax.numpy as jnp
from jax import lax
from jax.experimental import pallas as pl
from jax.experimental.pallas import tpu as pltpu

F32 = jnp.float32
BF16 = jnp.bfloat16
I32 = jnp.int32

D = 1024
DF = 512
NG = 4
DG = DF // NG
E = 32
TOPK = 4
DFF = 1024
N_MOD = 6
ALPHA = 1.702
LIMIT = 7.0
EPS = 1e-5
LANES = 128
SUBLANES = 8
ROW_TILE = D // LANES
FFT_N1 = 128
FFT_K1_TILE = SUBLANES
VMEM_LIMIT = 56 * 1024 * 1024
NEG = -1e30
DMA_GROUP = 8
COMBINE_ROWS = 32

assert ROW_TILE == SUBLANES


def _params(sem, vmem=VMEM_LIMIT):
    return pltpu.CompilerParams(dimension_semantics=sem, vmem_limit_bytes=vmem)


class _Geom:
    def __init__(self, bp, sp, bs, ss):
        assert bp > 0 and bs > 0
        self.bp, self.sp, self.bs, self.ss = bp, sp, bs, ss
        self.tp = bp * sp
        self.ts = bs * ss
        self.tt = self.tp + self.ts
        self.nseq = bp + bs

    def seq_of(self, t0):
        return jnp.where(t0 < self.tp, t0 // self.sp, self.bp + (t0 - self.tp) // self.ss)

    def offset_len(self, t0):
        in_p = t0 < self.tp
        off = jnp.where(in_p, t0 % self.sp, (t0 - self.tp) % self.ss)
        slen = jnp.where(in_p, self.sp, self.ss)
        return off, slen

    def split_specs(self, tm, width):
        npt = self.tp // tm
        return [pl.BlockSpec((tm, width), lambda i: (jnp.minimum(i, npt - 1), 0)),
                pl.BlockSpec((tm, width), lambda i: (jnp.maximum(i - npt, 0), 0))]


def _to_row_tiles(ref, val, rows):
    for s in range(ROW_TILE):
        ref[pl.ds(s, rows, stride=ROW_TILE), :] = val[:, s * LANES:(s + 1) * LANES]


def _from_row_tiles(ref, start, rows):
    return jnp.concatenate(
        [ref[pl.ds(start + s, rows, stride=ROW_TILE), :] for s in range(ROW_TILE)], axis=1)


def _mod_body(c_ref, w_ref, b_ref, o_ref):
    c = c_ref[...]
    s = c * jax.nn.sigmoid(c)
    o_ref[...] = jnp.dot(s, w_ref[...], preferred_element_type=F32,
                         precision=lax.Precision.HIGHEST) + b_ref[...]


def _modulation(c_pad, mod_w, mod_b):
    rows = c_pad.shape[0]
    return pl.pallas_call(
        _mod_body,
        grid=(N_MOD,),
        in_specs=[pl.BlockSpec((rows, D), lambda j: (0, 0)),
                  pl.BlockSpec((D, D), lambda j: (0, j)),
                  pl.BlockSpec((1, D), lambda j: (0, j))],
        out_specs=pl.BlockSpec((rows, D), lambda j: (0, j)),
        out_shape=jax.ShapeDtypeStruct((rows, N_MOD * D), F32),
        compiler_params=_params(("parallel",)),
        name="modulation",
    )(c_pad, mod_w, mod_b.reshape(1, N_MOD * D))


def _inproj_body(*refs, n_x, npt):
    x_refs = refs[:n_x]
    (mod_ref, g_ref, w_ref, cat_ref,
     gb_ref, p_ref, sa_ref, sf_ref, ap_ref, bp_ref, as_ref, bs_ref) = refs[n_x:]
    is_p = pl.program_id(0) < npt
    x = x_refs[0][...] if n_x == 1 else jnp.where(is_p, x_refs[0][...], x_refs[1][...])
    ms = jnp.mean(x * x, axis=-1, keepdims=True)
    xn = x * lax.rsqrt(ms + EPS) * g_ref[...]
    shift = mod_ref[:, 0:D]
    scale = mod_ref[:, D:2 * D]
    h = (xn * (1.0 + scale) + shift).astype(BF16)

    def proj(lo, hi):
        return jnp.dot(h, w_ref[:, lo:hi], preferred_element_type=F32)

    gb_ref[...] = proj(0, D).astype(BF16)
    p_ref[...] = (proj(D, 2 * D) * proj(2 * D, 3 * D)).astype(BF16)
    uf = proj(3 * D, 3 * D + DF).astype(BF16)
    abs_ = [jnp.dot(uf[:, g * DG:(g + 1) * DG], cat_ref[...], preferred_element_type=F32)
            for g in range(NG)]

    def put_ab(a_ref, b_ref):
        for g in range(NG):
            a_ref[g] = abs_[g][:, :DG]
            b_ref[g] = abs_[g][:, DG:]

    @pl.when(is_p)
    def _():
        put_ab(ap_ref, bp_ref)

    @pl.when(jnp.logical_not(is_p))
    def _():
        put_ab(as_ref, bs_ref)

    sa_ref[...] = jax.nn.sigmoid(proj(3 * D + DF, 4 * D + DF)).astype(BF16)
    sf_ref[...] = jax.nn.sigmoid(proj(4 * D + DF, 5 * D + DF)).astype(BF16)


def _inproj(geom, tm, xs, mod3, g1, w_in_bf, cat):
    tt = geom.tt
    npt = geom.tp // tm
    row = lambda i: (i, 0)
    full = lambda i: (0, 0)
    bf = lambda r, n: jax.ShapeDtypeStruct((r, n), BF16)
    x_specs = geom.split_specs(tm, D) if len(xs) == 2 else [pl.BlockSpec((tm, D), row)]
    ab_specs = [pl.BlockSpec((NG, tm, DG), lambda i: (0, jnp.minimum(i, npt - 1), 0)),
                pl.BlockSpec((NG, tm, DG), lambda i: (0, jnp.maximum(i - npt, 0), 0))]
    ab_shape = lambda r: jax.ShapeDtypeStruct((NG, r, DG), F32)
    return pl.pallas_call(
        functools.partial(_inproj_body, n_x=len(xs), npt=npt),
        grid=(tt // tm,),
        in_specs=x_specs + [
            pl.BlockSpec((None, 1, N_MOD * D), lambda i: (geom.seq_of(i * tm), 0, 0)),
            pl.BlockSpec((1, D), full),
            pl.BlockSpec(w_in_bf.shape, full),
            pl.BlockSpec(cat.shape, full)],
        out_specs=[pl.BlockSpec((tm, D), row), pl.BlockSpec((tm, D), row),
                   pl.BlockSpec((tm, D), row), pl.BlockSpec((tm, D), row),
                   ab_specs[0], ab_specs[0], ab_specs[1], ab_specs[1]],
        out_shape=[bf(tt, D), bf(tt, D), bf(tt, D), bf(tt, D),
                   ab_shape(geom.tp), ab_shape(geom.tp), ab_shape(geom.ts), ab_shape(geom.ts)],
        compiler_params=_params(("arbitrary",)),
        name="inproj",
    )(*xs, mod3, g1, w_in_bf, cat)


@functools.lru_cache(maxsize=None)
def _fft_consts(s):
    n1 = FFT_N1
    n2 = s // n1
    a = np.arange(n1)
    ang = 2.0 * np.pi * (np.outer(a, a) % n1) / n1
    c1, s1 = np.cos(ang), np.sin(ang)
    w1 = np.block([[c1, s1], [-s1, c1]])
    ang = 2.0 * np.pi * (np.outer(np.arange(n1), np.arange(n2)) % s) / s
    tc = np.repeat(np.cos(ang), LANES, axis=1)
    ts = np.repeat(np.sin(ang), LANES, axis=1)
    m = np.arange(n2)
    ang = 2.0 * np.pi * (np.outer(m, m) % n2) / n2
    c2, s2 = np.cos(ang), np.sin(ang)
    kt = FFT_K1_TILE
    wexp = np.zeros((n2, kt, 2, kt, n2))
    for j in range(kt):
        wexp[:, j, 0, j, :] = c2
        wexp[:, j, 1, j, :] = s2
    wexp = wexp.reshape(n2 * kt, 2 * kt * n2)
    return (np.asarray(w1, np.float32), np.asarray(tc, np.float32), np.asarray(ts, np.float32),
            np.asarray(wexp, np.float32))


@functools.lru_cache(maxsize=None)
def _chan_dft():
    a = np.arange(DG)
    ang = 2.0 * np.pi * (np.outer(a, a) % DG) / DG
    return np.asarray(np.concatenate([np.cos(ang), -np.sin(ang)], axis=1), np.float32)


def _fft1_body(a_ref, b_ref, w_ref, tc_ref, ts_ref, yr_ref, yi_ref, abuf, bbuf, yrbuf, yibuf, *, tn2):
    n1 = FFT_N1
    for g in range(NG):
        abuf[g] = a_ref[g].reshape(n1 * tn2, LANES)
        bbuf[g] = b_ref[g].reshape(n1 * tn2, LANES)

    for j in range(tn2):
        sl = pl.ds(j, n1, stride=tn2)
        a = jnp.concatenate([abuf[g, sl, :] for g in range(NG)], axis=1)
        b = jnp.concatenate([bbuf[g, sl, :] for g in range(NG)], axis=1)
        ab = jnp.concatenate([a, b], axis=0).astype(BF16)
        y = jnp.dot(w_ref[...], ab, preferred_element_type=F32)
        c = tc_ref[:, j * LANES:(j + 1) * LANES]
        s = ts_ref[:, j * LANES:(j + 1) * LANES]
        for g in range(NG):
            yr = y[:n1, g * LANES:(g + 1) * LANES]
            yi = y[n1:, g * LANES:(g + 1) * LANES]
            yrbuf[g, sl, :] = yr * c + yi * s
            yibuf[g, sl, :] = yi * c - yr * s

    for g in range(NG):
        yr_ref[g] = yrbuf[g].reshape(n1, tn2, LANES)
        yi_ref[g] = yibuf[g].reshape(n1, tn2, LANES)


def _fft2_body(yr_ref, yi_ref, w_ref, f_ref, *, n2, scale):
    yr = jnp.concatenate([yr_ref[g] for g in range(NG)], axis=1)
    yi = jnp.concatenate([yi_ref[g] for g in range(NG)], axis=1)
    rhs = jnp.concatenate([yr, yi], axis=0).astype(BF16)
    z = jnp.dot(w_ref[...], rhs, preferred_element_type=F32) * scale
    f_ref[...] = z.reshape(n2, FFT_K1_TILE, DF)


def _fourier(a, b, nseq, s):
    n1 = FFT_N1
    n2 = s // n1
    tn2 = SUBLANES
    kt = FFT_K1_TILE
    w1, tc, ts, wexp = _fft_consts(s)
    a5 = a.reshape(NG, nseq, n1, n2, DG)
    b5 = b.reshape(NG, nseq, n1, n2, DG)
    blk = pl.BlockSpec((NG, None, n1, tn2, DG), lambda q, j: (0, q, 0, j, 0))
    yr, yi = pl.pallas_call(
        functools.partial(_fft1_body, tn2=tn2),
        grid=(nseq, n2 // tn2),
        in_specs=[blk, blk,
                  pl.BlockSpec((2 * n1, 2 * n1), lambda q, j: (0, 0)),
                  pl.BlockSpec((n1, tn2 * LANES), lambda q, j: (0, j)),
                  pl.BlockSpec((n1, tn2 * LANES), lambda q, j: (0, j))],
        out_specs=[blk, blk],
        out_shape=[jax.ShapeDtypeStruct((NG, nseq, n1, n2, DG), F32)] * 2,
        scratch_shapes=[pltpu.VMEM((NG, n1 * tn2, DG), F32)] * 4,
        compiler_params=_params(("parallel", "parallel")),
        name="fft_stage1",
    )(a5, b5, jnp.asarray(w1, BF16), jnp.asarray(tc), jnp.asarray(ts))
    yr = yr.reshape(NG, nseq, n1 * n2, DG)
    yi = yi.reshape(NG, nseq, n1 * n2, DG)
    yblk = pl.BlockSpec((NG, None, kt * n2, DG), lambda q, j: (0, q, j, 0))
    f = pl.pallas_call(
        functools.partial(_fft2_body, n2=n2, scale=1.0 / math.sqrt(s * DG)),
        grid=(nseq, n1 // kt),
        in_specs=[yblk, yblk,
                  pl.BlockSpec(wexp.shape, lambda q, j: (0, 0))],
        out_specs=pl.BlockSpec((None, n2, kt, DF), lambda q, j: (q, 0, j, 0)),
        out_shape=jax.ShapeDtypeStruct((nseq, n2, n1, DF), F32),
        compiler_params=_params(("parallel", "parallel")),
        name="fft_stage2",
    )(yr, yi, jnp.asarray(wexp, BF16))
    return f.reshape(nseq * s, DF)


def _mixer_body(*refs, n_x, npt, tm, geom, halo):
    x_refs = refs[:n_x]
    (gb_ref, p_ref, pprev_ref, pnext_ref, sa_ref, sf_ref, fp_ref, fs_ref, mod_ref,
     cw_ref, cb_ref, wco_ref, wfo_ref, wo_ref, g2_ref, rw2_ref, rwh_ref, rb_ref, utri_ref,
     x1_ref, h2_ref, meta_ref, wts_ref, cnt_ref, carry) = refs[n_x:]
    i = pl.program_id(0)
    is_p = i < npt

    @pl.when(i == 0)
    def _():
        carry[...] = jnp.zeros_like(carry)

    x = x_refs[0][...] if n_x == 1 else jnp.where(is_p, x_refs[0][...], x_refs[1][...])
    f = jnp.where(is_p, fp_ref[...], fs_ref[...])

    off, slen = geom.offset_len(i * tm)
    keep_prev = jnp.where(off == 0, 0.0, 1.0)
    keep_next = jnp.where(off + tm == slen, 0.0, 1.0)
    p = p_ref[...].astype(F32)
    prev_row = pprev_ref[halo - 1:halo, :].astype(F32) * keep_prev
    next_row = pnext_ref[0:1, :].astype(F32) * keep_next
    row = lax.broadcasted_iota(I32, (tm, 1), 0)
    pm = jnp.where(row == 0, prev_row, pltpu.roll(p, 1, 0))
    pp = jnp.where(row == tm - 1, next_row, pltpu.roll(p, tm - 1, 0))
    conv = cw_ref[0:1, :] * pm + cw_ref[1:2, :] * p + cw_ref[2:3, :] * pp + cb_ref[...]
    ya_in = (gb_ref[...].astype(F32) * conv).astype(BF16)
    ya = jnp.dot(ya_in, wco_ref[...], preferred_element_type=F32)
    yf = jnp.dot(f.astype(BF16), wfo_ref[...], preferred_element_type=F32)
    merged = sa_ref[...].astype(F32) * ya + sf_ref[...].astype(F32) * yf
    mix = jnp.dot(merged.astype(BF16), wo_ref[...], preferred_element_type=F32)
    x1 = x + mod_ref[:, 2 * D:3 * D] * mix
    x1_ref[...] = x1

    ms = jnp.mean(x1 * x1, axis=-1, keepdims=True)
    xn = x1 * lax.rsqrt(ms + EPS) * g2_ref[...]
    h2 = xn * (1.0 + mod_ref[:, 4 * D:5 * D]) + mod_ref[:, 3 * D:4 * D]
    _to_row_tiles(h2_ref, h2, tm)

    hh = h2.astype(BF16)
    hl = (h2 - hh.astype(F32)).astype(BF16)
    two = jnp.dot(hh, rw2_ref[...], preferred_element_type=F32)
    logits = (two[:, :LANES] + two[:, LANES:]
              + jnp.dot(hl, rwh_ref[...], preferred_element_type=F32)) + rb_ref[...]

    l = jnp.transpose(logits)[0:E, :]
    sub = lax.broadcasted_iota(I32, (E, tm), 0)
    vals, idxs = [], []
    free = jnp.ones((E, tm), jnp.bool_)
    for _ in range(TOPK):
        m = jnp.max(jnp.where(free, l, -jnp.inf), axis=0, keepdims=True)
        ix = jnp.min(jnp.where(free & (l == m), sub, E), axis=0, keepdims=True)
        vals.append(m)
        idxs.append(ix)
        free = free & (sub != ix)
    exps = [jnp.exp(v - vals[0]) for v in vals]
    denom = exps[0] + exps[1] + exps[2] + exps[3]
    ws = [e / denom for e in exps]

    onehot = jnp.zeros((E, tm), F32)
    for ix in idxs:
        onehot = onehot + jnp.where(sub == ix, 1.0, 0.0)
    before = jnp.dot(onehot.astype(BF16), utri_ref[...], preferred_element_type=F32) + carry[:, 0:1]
    ranks = [jnp.sum(jnp.where(sub == ix, before, 0.0), axis=0, keepdims=True).astype(I32)
             for ix in idxs]
    carry[...] = carry[...] + jnp.sum(onehot, axis=1, keepdims=True)
    cnt_ref[...] = carry[...]

    sub8 = lax.broadcasted_iota(I32, (2 * TOPK, tm), 0)
    meta = jnp.zeros((2 * TOPK, tm), I32)
    subw = lax.broadcasted_iota(I32, (LANES, tm), 0)
    wt = jnp.zeros((LANES, tm), F32)
    for k in range(TOPK):
        meta = jnp.where(sub8 == k, idxs[k], meta)
        meta = jnp.where(sub8 == TOPK + k, ranks[k], meta)
        wt = jnp.where(subw == k, ws[k], wt)
    meta_ref[...] = meta
    wts_ref[...] = jnp.transpose(wt)


def _mixer(geom, tm, xs, gb, p, sa, sf, f_p, f_s, mod3, conv_w, conv_b, wco, wfo, wo, g2,
           rw2, rwh, rb, utri):
    tt = geom.tt
    halo = 16
    hb = tm // halo
    nhalo = tt // halo
    row = lambda i: (i, 0)
    full = lambda i: (0, 0)
    x_specs = geom.split_specs(tm, D) if len(xs) == 2 else [pl.BlockSpec((tm, D), row)]
    body = functools.partial(_mixer_body, n_x=len(xs), npt=geom.tp // tm, tm=tm, geom=geom, halo=halo)
    return pl.pallas_call(
        body,
        grid=(tt // tm,),
        in_specs=x_specs + [
            pl.BlockSpec((tm, D), row),
            pl.BlockSpec((tm, D), row),
            pl.BlockSpec((halo, D), lambda i: (jnp.maximum(i * hb - 1, 0), 0)),
            pl.BlockSpec((halo, D), lambda i: (jnp.minimum((i + 1) * hb, nhalo - 1), 0)),
            pl.BlockSpec((tm, D), row),
            pl.BlockSpec((tm, D), row)] + geom.split_specs(tm, DF) + [
            pl.BlockSpec((None, 1, N_MOD * D), lambda i: (geom.seq_of(i * tm), 0, 0)),
            pl.BlockSpec((3, D), full),
            pl.BlockSpec((1, D), full),
            pl.BlockSpec((D, D), full),
            pl.BlockSpec((DF, D), full),
            pl.BlockSpec((D, D), full),
            pl.BlockSpec((1, D), full),
            pl.BlockSpec((D, 2 * LANES), full),
            pl.BlockSpec((D, LANES), full),
            pl.BlockSpec((1, LANES), full),
            pl.BlockSpec((tm, tm), full)],
        out_specs=[pl.BlockSpec((tm, D), row),
                   pl.BlockSpec((tm * ROW_TILE, LANES), row),
                   pl.BlockSpec((2 * TOPK, tm), lambda i: (0, i)),
                   pl.BlockSpec((tm, LANES), row),
                   pl.BlockSpec((E, LANES), full)],
        out_shape=[jax.ShapeDtypeStruct((tt, D), F32),
                   jax.ShapeDtypeStruct((tt * ROW_TILE, LANES), F32),
                   jax.ShapeDtypeStruct((2 * TOPK, tt), I32),
                   jax.ShapeDtypeStruct((tt, LANES), F32),
                   jax.ShapeDtypeStruct((E, LANES), F32)],
        scratch_shapes=[pltpu.VMEM((E, LANES), F32)],
        compiler_params=_params(("arbitrary",)),
        name="mixer_router",
    )(*xs, gb, p, p, p, sa, sf, f_p, f_s, mod3, conv_w, conv_b, wco, wfo, wo, g2, rw2, rwh, rb, utri)


def _row_tile(ref, r):
    return ref.at[pl.ds(pl.multiple_of(r * ROW_TILE, ROW_TILE), ROW_TILE), :]


def _dest_base(g):
    per_chunk = LANES // DMA_GROUP
    chunk = lax.shift_right_logical(g, per_chunk.bit_length() - 1)
    return chunk * (TOPK * LANES) + (g & (per_chunk - 1)) * DMA_GROUP


def _chunk_dest(dest):
    tt = dest.shape[1]
    return dest.reshape(TOPK, tt // LANES, LANES).transpose(1, 0, 2).reshape(-1)


def _dispatch_body(dest_ref, h_ref, xs_hbm, sem, *, tq):
    def issue(g, c):
        base = _dest_base(g)
        for u in range(DMA_GROUP):
            t = g * DMA_GROUP + u
            for k in range(TOPK):
                d = dest_ref[base + (k * LANES + u)]
                pltpu.make_async_copy(_row_tile(h_ref, t), _row_tile(xs_hbm, d),
                                      sem).start(priority=(u * TOPK + k) % 2)
        return c

    lax.fori_loop(0, tq // DMA_GROUP, issue, 0)
    n = tq * TOPK * ROW_TILE
    pltpu.make_async_copy(xs_hbm.at[pl.ds(0, n), :], xs_hbm.at[pl.ds(0, n), :], sem).wait()


def _dispatch(tq, dest, h2r, n_rows):
    tt = h2r.shape[0] // ROW_TILE
    return pl.pallas_call(
        functools.partial(_dispatch_body, tq=tq),
        grid=(tt // tq,),
        in_specs=[pl.BlockSpec((TOPK * tq,), lambda i: (i,), memory_space=pltpu.SMEM),
                  pl.BlockSpec((tq * ROW_TILE, LANES), lambda i: (i, 0))],
        out_specs=pl.BlockSpec(memory_space=pl.ANY),
        out_shape=jax.ShapeDtypeStruct((n_rows * ROW_TILE, LANES), F32),
        scratch_shapes=[pltpu.SemaphoreType.DMA],
        compiler_params=_params(("arbitrary",)),
        name="moe_dispatch",
    )(dest, h2r)


def _ffn_body(be_ref, nv_ref, nu_ref, xs_ref, wgu_ref, bgu_ref, wdn_ref, bdn_ref, os_ref,
              wgu_bf, wdn_bf, *, bm):
    i = pl.program_id(0)

    @pl.when(i < nu_ref[0])
    def _():
        @pl.when((i == 0) | (be_ref[i] != be_ref[jnp.maximum(i - 1, 0)]))
        def _():
            wgu_bf[...] = wgu_ref[...].astype(BF16)
            wdn_bf[...] = wdn_ref[...].astype(BF16)

        x = _from_row_tiles(xs_ref, 0, bm)
        valid = lax.broadcasted_iota(I32, (bm, 1), 0) < nv_ref[i]
        x = jnp.where(valid, x, 0.0).astype(BF16)
        gu = jnp.dot(x, wgu_bf[...], preferred_element_type=F32) + bgu_ref[...]
        glu = jnp.minimum(gu[:, :DFF], LIMIT)
        lin = jnp.clip(gu[:, DFF:], -LIMIT, LIMIT)
        act = glu * jax.nn.sigmoid(ALPHA * glu) * (lin + 1.0)
        out = jnp.dot(act.astype(BF16), wdn_bf[...], preferred_element_type=F32) + bdn_ref[...]
        _to_row_tiles(os_ref, out, bm)


def _expert_ffn(bm, layer, block_expert, n_valid, n_used, xs, wgu, bgu, wdn, bdn):
    n_blocks = xs.shape[0] // (bm * ROW_TILE)
    rows = lambda i, be, nv, nu: (jnp.minimum(i, nu[0] - 1), 0)
    wsel = lambda i, be, nv, nu: (layer, be[i], 0, 0)
    grid_spec = pltpu.PrefetchScalarGridSpec(
        num_scalar_prefetch=3,
        grid=(n_blocks,),
        in_specs=[pl.BlockSpec((bm * ROW_TILE, LANES), rows),
                  pl.BlockSpec((None, None, D, 2 * DFF), wsel),
                  pl.BlockSpec((None, None, 1, 2 * DFF), wsel),
                  pl.BlockSpec((None, None, DFF, D), wsel),
                  pl.BlockSpec((None, None, 1, D), wsel)],
        out_specs=pl.BlockSpec((bm * ROW_TILE, LANES), rows),
        scratch_shapes=[pltpu.VMEM((D, 2 * DFF), BF16), pltpu.VMEM((DFF, D), BF16)],
    )
    return pl.pallas_call(
        functools.partial(_ffn_body, bm=bm),
        grid_spec=grid_spec,
        out_shape=jax.ShapeDtypeStruct(xs.shape, F32),
        compiler_params=_params(("arbitrary",)),
        name="expert_ffn",
    )(block_expert, n_valid, n_used, xs, wgu, bgu, wdn, bdn)


def _combine_body(dcur_ref, dnxt_ref, os_hbm, w_ref, x1_ref, mod_ref, g_ref, *rest, tq, final, npt):
    out_refs, (stage, sems) = rest[:-2], rest[-2:]
    i = pl.program_id(0)
    n = pl.num_programs(0)

    def issue(dref, slot):
        def grp(g, c):
            base = _dest_base(g)
            row0 = g * (DMA_GROUP * ROW_TILE)
            for u in range(DMA_GROUP):
                for k in range(TOPK):
                    r = pl.multiple_of(row0 + (k * tq + u) * ROW_TILE, ROW_TILE)
                    pltpu.make_async_copy(_row_tile(os_hbm, dref[base + (k * LANES + u)]),
                                          stage.at[slot, pl.ds(r, ROW_TILE), :],
                                          sems.at[slot]).start(priority=(u * TOPK + k) % 2)
            return c
        lax.fori_loop(0, tq // DMA_GROUP, grp, 0)

    def consume(slot, out_ref):
        nrow = TOPK * tq * ROW_TILE
        pltpu.make_async_copy(os_hbm.at[pl.ds(0, nrow), :], stage.at[slot], sems.at[slot]).wait()
        cr = COMBINE_ROWS
        for c in range(tq // cr):
            rows = slice(c * cr, (c + 1) * cr)
            wk = [jnp.broadcast_to(w_ref[rows, k:k + 1], (cr, LANES)) for k in range(TOPK)]
            ssq = jnp.zeros((cr, 1), F32)
            for s in range(ROW_TILE):
                cols = slice(s * LANES, (s + 1) * LANES)
                y = wk[0] * stage[slot, pl.ds(c * cr * ROW_TILE + s, cr, stride=ROW_TILE), :]
                for k in range(1, TOPK):
                    start = (k * tq + c * cr) * ROW_TILE + s
                    y = y + wk[k] * stage[slot, pl.ds(start, cr, stride=ROW_TILE), :]
                x2 = x1_ref[rows, cols] + mod_ref[:, 5 * D + s * LANES:5 * D + (s + 1) * LANES] * y
                if final:
                    ssq = ssq + jnp.sum(x2 * x2, axis=-1, keepdims=True)
                out_ref[rows, cols] = x2
            if final:
                inv = lax.rsqrt(ssq * (1.0 / D) + EPS)
                out_ref[rows, :] = out_ref[rows, :] * inv * g_ref[...]

    @pl.when(i == 0)
    def _():
        issue(dcur_ref, 0)

    for slot in range(2):
        @pl.when((i + 1 < n) & ((i + 1) % 2 == slot))
        def _():
            issue(dnxt_ref, slot)

    for slot in range(2):
        if not final:
            @pl.when(i % 2 == slot)
            def _():
                consume(slot, out_refs[0])
        else:
            @pl.when((i % 2 == slot) & (i < npt))
            def _():
                consume(slot, out_refs[0])

            @pl.when((i % 2 == slot) & (i >= npt))
            def _():
                consume(slot, out_refs[1])


def _combine(geom, tq, dest, os_rows, wts, x1, mod3, g_final, final):
    tt = geom.tt
    nt = tt // tq
    if final:
        out_specs = geom.split_specs(tq, D)
        out_shape = [jax.ShapeDtypeStruct((geom.tp, D), F32), jax.ShapeDtypeStruct((geom.ts, D), F32)]
    else:
        out_specs = [pl.BlockSpec((tq, D), lambda i: (i, 0))]
        out_shape = [jax.ShapeDtypeStruct((tt, D), F32)]
    return pl.pallas_call(
        functools.partial(_combine_body, tq=tq, final=final, npt=geom.tp // tq),
        grid=(nt,),
        in_specs=[pl.BlockSpec((TOPK * tq,), lambda i: (i,), memory_space=pltpu.SMEM),
                  pl.BlockSpec((TOPK * tq,), lambda i: (jnp.minimum(i + 1, nt - 1),),
                               memory_space=pltpu.SMEM),
                  pl.BlockSpec(memory_space=pl.ANY),
                  pl.BlockSpec((tq, LANES), lambda i: (i, 0)),
                  pl.BlockSpec((tq, D), lambda i: (i, 0)),
                  pl.BlockSpec((None, 1, N_MOD * D), lambda i: (geom.seq_of(i * tq), 0, 0)),
                  pl.BlockSpec((1, D), lambda i: (0, 0))],
        out_specs=out_specs,
        out_shape=out_shape,
        scratch_shapes=[pltpu.VMEM((2, TOPK * tq * ROW_TILE, LANES), F32),
                        pltpu.SemaphoreType.DMA((2,))],
        compiler_params=_params(("arbitrary",)),
        name="moe_combine",
    )(dest, dest, os_rows, wts, x1, mod3, g_final)


def _route_plan(meta_t, counts_f, bm, n_blocks):
    idx = meta_t[0:TOPK]
    rank = meta_t[TOPK:2 * TOPK]
    counts = counts_f[:, 0].astype(I32)
    padded = (counts + bm - 1) // bm * bm
    pad_end = jnp.cumsum(padded)
    pad_start = pad_end - padded
    dest = rank
    for e in range(E):
        dest = dest + jnp.where(idx == e, pad_start[e], 0)
    n_used = (pad_end[E - 1] // bm).astype(I32)
    blk = jnp.arange(n_blocks, dtype=I32)
    be = jnp.minimum(jnp.sum((pad_end[None, :] <= (blk * bm)[:, None]).astype(I32), axis=1), E - 1)
    be = jnp.where(blk < n_used, be, be[n_used - 1]).astype(I32)
    n_valid = jnp.clip((pad_start + counts)[be] - blk * bm, 0, bm).astype(I32)
    return _chunk_dest(dest), be, n_valid, n_used.reshape(1)


def _trunk(geom, x_p, x_s, c_all, mod_w, mod_b, norm1_g, norm2_g, w_in, conv_w, conv_b, w_conv_out,
           w_fourier_out, w_o, router_w, router_b, w_gate_up, b_gate_up, w_down, b_down, final_g,
           tm=512, tq_dispatch=2048, tq_combine=512, bm=512):
    depth = mod_w.shape[0]
    tt = geom.tt
    n_blocks = -(-tt * TOPK // bm) + E
    n_rows = n_blocks * bm
    cat = jnp.asarray(_chan_dft(), BF16)
    utri = jnp.asarray(np.triu(np.ones((tm, tm), np.float32), 1), BF16)
    c_pad = jnp.zeros((-(-geom.nseq // 8) * 8, D), F32).at[:geom.nseq].set(c_all)
    xs = (x_p, x_s)
    for l in range(depth):
        mod = _modulation(c_pad, mod_w[l], mod_b[l])
        mod3 = mod.reshape(mod.shape[0], 1, N_MOD * D)
        gb, p, sa, sf, a_p, b_p, a_s, b_s = _inproj(geom, tm, xs, mod3, norm1_g[l].reshape(1, D),
                                                    w_in[l].astype(BF16), cat)
        f_p = _fourier(a_p, b_p, geom.bp, geom.sp)
        f_s = _fourier(a_s, b_s, geom.bs, geom.ss)
        rw = jnp.zeros((D, LANES), F32).at[:, :E].set(router_w[l])
        rwh = rw.astype(BF16)
        rw2 = jnp.concatenate([rwh, (rw - rwh.astype(F32)).astype(BF16)], axis=1)
        rb = jnp.full((1, LANES), NEG, F32).at[0, :E].set(router_b[l])
        x1, h2r, meta_t, wts, counts = _mixer(
            geom, tm, xs, gb, p, sa, sf, f_p, f_s, mod3, conv_w[l], conv_b[l].reshape(1, D),
            w_conv_out[l].astype(BF16), w_fourier_out[l].astype(BF16), w_o[l].astype(BF16),
            norm2_g[l].reshape(1, D), rw2, rwh, rb, utri)
        dest, block_expert, n_valid, n_used = _route_plan(meta_t, counts, bm, n_blocks)
        xrows = _dispatch(tq_dispatch, dest, h2r, n_rows)
        os_rows = _expert_ffn(bm, l, block_expert, n_valid, n_used, xrows,
                              w_gate_up, b_gate_up.reshape(depth, E, 1, 2 * DFF),
                              w_down, b_down.reshape(depth, E, 1, D))
        xs = tuple(_combine(geom, tq_combine, dest, os_rows, wts, x1, mod3, final_g.reshape(1, D),
                            final=(l == depth - 1)))
    return xs


def kernel(x_prompt, x_sample, c_prompt, c_sample, mod_w, mod_b, norm1_g, norm2_g, w_in, conv_w,
           conv_b, w_conv_out, w_fourier_out, w_o, router_w, router_b, w_gate_up, b_gate_up,
           w_down, b_down, final_g):
    bp, sp, _ = x_prompt.shape
    bs, ss, _ = x_sample.shape
    geom = _Geom(bp, sp, bs, ss)
    c_all = jnp.concatenate([c_prompt, c_sample], axis=0)
    y_p, y_s = _trunk(geom, x_prompt.reshape(bp * sp, D), x_sample.reshape(bs * ss, D), c_all,
                      mod_w, mod_b, norm1_g, norm2_g, w_in, conv_w, conv_b, w_conv_out,
                      w_fourier_out, w_o, router_w, router_b, w_gate_up, b_gate_up, w_down, b_down,
                      final_g)
    return (y_p.reshape(bp, sp, D), y_s.reshape(bs, ss, D))
```

```python
import functools
import math

import numpy as np
import jax
import jax.numpy as jnp
from jax import lax
from jax.experimental import pallas as pl
from jax.experimental.pallas import tpu as pltpu

F32 = jnp.float32
BF16 = jnp.bfloat16
I32 = jnp.int32

D = 1024
DF = 512
NG = 4
DG = DF // NG
E = 32
TOPK = 4
DFF = 1024
N_MOD = 6
ALPHA = 1.702
LIMIT = 7.0
EPS = 1e-5
LANES = 128
SUBLANES = 8
ROW_TILE = D // LANES
FFT_N1 = 128
FFT_K1_TILE = SUBLANES
VMEM_LIMIT = 56 * 1024 * 1024
NEG = -1e30
DMA_GROUP = 8
COMBINE_ROWS = 8

assert ROW_TILE == SUBLANES


def _params(sem, vmem=VMEM_LIMIT):
    return pltpu.CompilerParams(dimension_semantics=sem, vmem_limit_bytes=vmem)


class _Geom:
    def __init__(self, bp, sp, bs, ss):
        assert bp > 0 and bs > 0
        self.bp, self.sp, self.bs, self.ss = bp, sp, bs, ss
        self.tp = bp * sp
        self.ts = bs * ss
        self.tt = self.tp + self.ts
        self.nseq = bp + bs

    def seq_of(self, t0):
        return jnp.where(t0 < self.tp, t0 // self.sp, self.bp + (t0 - self.tp) // self.ss)

    def offset_len(self, t0):
        in_p = t0 < self.tp
        off = jnp.where(in_p, t0 % self.sp, (t0 - self.tp) % self.ss)
        slen = jnp.where(in_p, self.sp, self.ss)
        return off, slen

    def split_specs(self, tm, width):
        npt = self.tp // tm
        return [pl.BlockSpec((tm, width), lambda i: (jnp.minimum(i, npt - 1), 0)),
                pl.BlockSpec((tm, width), lambda i: (jnp.maximum(i - npt, 0), 0))]


def _to_row_tiles(ref, val, rows):
    for s in range(ROW_TILE):
        ref[pl.ds(s, rows, stride=ROW_TILE), :] = val[:, s * LANES:(s + 1) * LANES]


def _from_row_tiles(ref, start, rows):
    return jnp.concatenate(
        [ref[pl.ds(start + s, rows, stride=ROW_TILE), :] for s in range(ROW_TILE)], axis=1)


def _mod_body(c_ref, w_ref, b_ref, o_ref):
    c = c_ref[...]
    s = c * jax.nn.sigmoid(c)
    o_ref[...] = jnp.dot(s, w_ref[...], preferred_element_type=F32,
                         precision=lax.Precision.HIGHEST) + b_ref[...]


def _modulation(c_pad, mod_w, mod_b):
    rows = c_pad.shape[0]
    return pl.pallas_call(
        _mod_body,
        grid=(N_MOD,),
        in_specs=[pl.BlockSpec((rows, D), lambda j: (0, 0)),
                  pl.BlockSpec((D, D), lambda j: (0, j)),
                  pl.BlockSpec((1, D), lambda j: (0, j))],
        out_specs=pl.BlockSpec((rows, D), lambda j: (0, j)),
        out_shape=jax.ShapeDtypeStruct((rows, N_MOD * D), F32),
        compiler_params=_params(("parallel",)),
        name="modulation",
    )(c_pad, mod_w, mod_b.reshape(1, N_MOD * D))


def _inproj_body(*refs, n_x, npt):
    x_refs = refs[:n_x]
    (mod_ref, g_ref, w_ref, cat_ref,
     gb_ref, p_ref, sa_ref, sf_ref, ap_ref, bp_ref, as_ref, bs_ref) = refs[n_x:]
    is_p = pl.program_id(0) < npt
    x = x_refs[0][...] if n_x == 1 else jnp.where(is_p, x_refs[0][...], x_refs[1][...])
    ms = jnp.mean(x * x, axis=-1, keepdims=True)
    xn = x * lax.rsqrt(ms + EPS) * g_ref[...]
    shift = mod_ref[:, 0:D]
    scale = mod_ref[:, D:2 * D]
    h = (xn * (1.0 + scale) + shift).astype(BF16)

    def proj(lo, hi):
        return jnp.dot(h, w_ref[:, lo:hi], preferred_element_type=F32)

    gb_ref[...] = proj(0, D).astype(BF16)
    p_ref[...] = (proj(D, 2 * D) * proj(2 * D, 3 * D)).astype(BF16)
    uf = proj(3 * D, 3 * D + DF).astype(BF16)
    abs_ = [jnp.dot(uf[:, g * DG:(g + 1) * DG], cat_ref[...], preferred_element_type=F32)
            for g in range(NG)]

    def put_ab(a_ref, b_ref):
        for g in range(NG):
            a_ref[g] = abs_[g][:, :DG]
            b_ref[g] = abs_[g][:, DG:]

    @pl.when(is_p)
    def _():
        put_ab(ap_ref, bp_ref)

    @pl.when(jnp.logical_not(is_p))
    def _():
        put_ab(as_ref, bs_ref)

    sa_ref[...] = jax.nn.sigmoid(proj(3 * D + DF, 4 * D + DF)).astype(BF16)
    sf_ref[...] = jax.nn.sigmoid(proj(4 * D + DF, 5 * D + DF)).astype(BF16)


def _inproj(geom, tm, xs, mod3, g1, w_in_bf, cat):
    tt = geom.tt
    npt = geom.tp // tm
    row = lambda i: (i, 0)
    full = lambda i: (0, 0)
    bf = lambda r, n: jax.ShapeDtypeStruct((r, n), BF16)
    x_specs = geom.split_specs(tm, D) if len(xs) == 2 else [pl.BlockSpec((tm, D), row)]
    ab_specs = [pl.BlockSpec((NG, tm, DG), lambda i: (0, jnp.minimum(i, npt - 1), 0)),
                pl.BlockSpec((NG, tm, DG), lambda i: (0, jnp.maximum(i - npt, 0), 0))]
    ab_shape = lambda r: jax.ShapeDtypeStruct((NG, r, DG), F32)
    return pl.pallas_call(
        functools.partial(_inproj_body, n_x=len(xs), npt=npt),
        grid=(tt // tm,),
        in_specs=x_specs + [
            pl.BlockSpec((None, 1, N_MOD * D), lambda i: (geom.seq_of(i * tm), 0, 0)),
            pl.BlockSpec((1, D), full),
            pl.BlockSpec(w_in_bf.shape, full),
            pl.BlockSpec(cat.shape, full)],
        out_specs=[pl.BlockSpec((tm, D), row), pl.BlockSpec((tm, D), row),
                   pl.BlockSpec((tm, D), row), pl.BlockSpec((tm, D), row),
                   ab_specs[0], ab_specs[0], ab_specs[1], ab_specs[1]],
        out_shape=[bf(tt, D), bf(tt, D), bf(tt, D), bf(tt, D),
                   ab_shape(geom.tp), ab_shape(geom.tp), ab_shape(geom.ts), ab_shape(geom.ts)],
        compiler_params=_params(("arbitrary",)),
        name="inproj",
    )(*xs, mod3, g1, w_in_bf, cat)


@functools.lru_cache(maxsize=None)
def _fft_consts(s):
    n1 = FFT_N1
    n2 = s // n1
    a = np.arange(n1)
    ang = 2.0 * np.pi * (np.outer(a, a) % n1) / n1
    c1, s1 = np.cos(ang), np.sin(ang)
    w1 = np.block([[c1, s1], [-s1, c1]])
    ang = 2.0 * np.pi * (np.outer(np.arange(n1), np.arange(n2)) % s) / s
    tc = np.repeat(np.cos(ang), LANES, axis=1)
    ts = np.repeat(np.sin(ang), LANES, axis=1)
    m = np.arange(n2)
    ang = 2.0 * np.pi * (np.outer(m, m) % n2) / n2
    c2, s2 = np.cos(ang), np.sin(ang)
    kt = FFT_K1_TILE
    wexp = np.zeros((n2, kt, 2, kt, n2))
    for j in range(kt):
        wexp[:, j, 0, j, :] = c2
        wexp[:, j, 1, j, :] = s2
    wexp = wexp.reshape(n2 * kt, 2 * kt * n2)
    return (np.asarray(w1, np.float32), np.asarray(tc, np.float32), np.asarray(ts, np.float32),
            np.asarray(wexp, np.float32))


@functools.lru_cache(maxsize=None)
def _chan_dft():
    a = np.arange(DG)
    ang = 2.0 * np.pi * (np.outer(a, a) % DG) / DG
    return np.asarray(np.concatenate([np.cos(ang), -np.sin(ang)], axis=1), np.float32)


def _fft1_body(a_ref, b_ref, w_ref, tc_ref, ts_ref, yr_ref, yi_ref, abuf, bbuf, yrbuf, yibuf, *, tn2):
    n1 = FFT_N1
    for g in range(NG):
        abuf[g] = a_ref[g].reshape(n1 * tn2, LANES)
        bbuf[g] = b_ref[g].reshape(n1 * tn2, LANES)

    for j in range(tn2):
        sl = pl.ds(j, n1, stride=tn2)
        a = jnp.concatenate([abuf[g, sl, :] for g in range(NG)], axis=1)
        b = jnp.concatenate([bbuf[g, sl, :] for g in range(NG)], axis=1)
        ab = jnp.concatenate([a, b], axis=0).astype(BF16)
        y = jnp.dot(w_ref[...], ab, preferred_element_type=F32)
        c = tc_ref[:, j * LANES:(j + 1) * LANES]
        s = ts_ref[:, j * LANES:(j + 1) * LANES]
        for g in range(NG):
            yr = y[:n1, g * LANES:(g + 1) * LANES]
            yi = y[n1:, g * LANES:(g + 1) * LANES]
            yrbuf[g, sl, :] = yr * c + yi * s
            yibuf[g, sl, :] = yi * c - yr * s

    for g in range(NG):
        yr_ref[g] = yrbuf[g].reshape(n1, tn2, LANES)
        yi_ref[g] = yibuf[g].reshape(n1, tn2, LANES)


def _fft2_body(yr_ref, yi_ref, w_ref, f_ref, *, n2, scale):
    yr = jnp.concatenate([yr_ref[g] for g in range(NG)], axis=1)
    yi = jnp.concatenate([yi_ref[g] for g in range(NG)], axis=1)
    rhs = jnp.concatenate([yr, yi], axis=0).astype(BF16)
    z = jnp.dot(w_ref[...], rhs, preferred_element_type=F32) * scale
    f_ref[...] = z.reshape(n2, FFT_K1_TILE, DF)


def _fourier(a, b, nseq, s):
    n1 = FFT_N1
    n2 = s // n1
    tn2 = SUBLANES
    kt = FFT_K1_TILE
    w1, tc, ts, wexp = _fft_consts(s)
    a5 = a.reshape(NG, nseq, n1, n2, DG)
    b5 = b.reshape(NG, nseq, n1, n2, DG)
    blk = pl.BlockSpec((NG, None, n1, tn2, DG), lambda q, j: (0, q, 0, j, 0))
    yr, yi = pl.pallas_call(
        functools.partial(_fft1_body, tn2=tn2),
        grid=(nseq, n2 // tn2),
        in_specs=[blk, blk,
                  pl.BlockSpec((2 * n1, 2 * n1), lambda q, j: (0, 0)),
                  pl.BlockSpec((n1, tn2 * LANES), lambda q, j: (0, j)),
                  pl.BlockSpec((n1, tn2 * LANES), lambda q, j: (0, j))],
        out_specs=[blk, blk],
        out_shape=[jax.ShapeDtypeStruct((NG, nseq, n1, n2, DG), F32)] * 2,
        scratch_shapes=[pltpu.VMEM((NG, n1 * tn2, DG), F32)] * 4,
        compiler_params=_params(("parallel", "parallel")),
        name="fft_stage1",
    )(a5, b5, jnp.asarray(w1, BF16), jnp.asarray(tc), jnp.asarray(ts))
    yr = yr.reshape(NG, nseq, n1 * n2, DG)
    yi = yi.reshape(NG, nseq, n1 * n2, DG)
    yblk = pl.BlockSpec((NG, None, kt * n2, DG), lambda q, j: (0, q, j, 0))
    f = pl.pallas_call(
        functools.partial(_fft2_body, n2=n2, scale=1.0 / math.sqrt(s * DG)),
        grid=(nseq, n1 // kt),
        in_specs=[yblk, yblk,
                  pl.BlockSpec(wexp.shape, lambda q, j: (0, 0))],
        out_specs=pl.BlockSpec((None, n2, kt, DF), lambda q, j: (q, 0, j, 0)),
        out_shape=jax.ShapeDtypeStruct((nseq, n2, n1, DF), F32),
        compiler_params=_params(("parallel", "parallel")),
        name="fft_stage2",
    )(yr, yi, jnp.asarray(wexp, BF16))
    return f.reshape(nseq * s, DF)


def _mixer_body(*refs, n_x, npt, tm, geom, halo):
    x_refs = refs[:n_x]
    (gb_ref, p_ref, pprev_ref, pnext_ref, sa_ref, sf_ref, fp_ref, fs_ref, mod_ref,
     cw_ref, cb_ref, wco_ref, wfo_ref, wo_ref, g2_ref, rw2_ref, rwh_ref, rb_ref, utri_ref,
     x1_ref, h2_ref, meta_ref, wts_ref, cnt_ref, carry) = refs[n_x:]
    i = pl.program_id(0)
    is_p = i < npt

    @pl.when(i == 0)
    def _():
        carry[...] = jnp.zeros_like(carry)

    x = x_refs[0][...] if n_x == 1 else jnp.where(is_p, x_refs[0][...], x_refs[1][...])
    f = jnp.where(is_p, fp_ref[...], fs_ref[...])

    off, slen = geom.offset_len(i * tm)
    keep_prev = jnp.where(off == 0, 0.0, 1.0)
    keep_next = jnp.where(off + tm == slen, 0.0, 1.0)
    p = p_ref[...].astype(F32)
    prev_row = pprev_ref[halo - 1:halo, :].astype(F32) * keep_prev
    next_row = pnext_ref[0:1, :].astype(F32) * keep_next
    row = lax.broadcasted_iota(I32, (tm, 1), 0)
    pm = jnp.where(row == 0, prev_row, pltpu.roll(p, 1, 0))
    pp = jnp.where(row == tm - 1, next_row, pltpu.roll(p, tm - 1, 0))
    conv = cw_ref[0:1, :] * pm + cw_ref[1:2, :] * p + cw_ref[2:3, :] * pp + cb_ref[...]
    ya_in = (gb_ref[...].astype(F32) * conv).astype(BF16)
    ya = jnp.dot(ya_in, wco_ref[...], preferred_element_type=F32)
    yf = jnp.dot(f.astype(BF16), wfo_ref[...], preferred_element_type=F32)
    merged = sa_ref[...].astype(F32) * ya + sf_ref[...].astype(F32) * yf
    mix = jnp.dot(merged.astype(BF16), wo_ref[...], preferred_element_type=F32)
    x1 = x + mod_ref[:, 2 * D:3 * D] * mix
    x1_ref[...] = x1

    ms = jnp.mean(x1 * x1, axis=-1, keepdims=True)
    xn = x1 * lax.rsqrt(ms + EPS) * g2_ref[...]
    h2 = xn * (1.0 + mod_ref[:, 4 * D:5 * D]) + mod_ref[:, 3 * D:4 * D]
    _to_row_tiles(h2_ref, h2, tm)

    hh = h2.astype(BF16)
    hl = (h2 - hh.astype(F32)).astype(BF16)
    two = jnp.dot(hh, rw2_ref[...], preferred_element_type=F32)
    logits = (two[:, :LANES] + two[:, LANES:]
              + jnp.dot(hl, rwh_ref[...], preferred_element_type=F32)) + rb_ref[...]

    l = jnp.transpose(logits)[0:E, :]
    sub = lax.broadcasted_iota(I32, (E, tm), 0)
    vals, idxs = [], []
    free = jnp.ones((E, tm), jnp.bool_)
    for _ in range(TOPK):
        m = jnp.max(jnp.where(free, l, -jnp.inf), axis=0, keepdims=True)
        ix = jnp.min(jnp.where(free & (l == m), sub, E), axis=0, keepdims=True)
        vals.append(m)
        idxs.append(ix)
        free = free & (sub != ix)
    exps = [jnp.exp(v - vals[0]) for v in vals]
    denom = exps[0] + exps[1] + exps[2] + exps[3]
    ws = [e / denom for e in exps]

    onehot = jnp.zeros((E, tm), F32)
    for ix in idxs:
        onehot = onehot + jnp.where(sub == ix, 1.0, 0.0)
    before = jnp.dot(onehot.astype(BF16), utri_ref[...], preferred_element_type=F32) + carry[:, 0:1]
    ranks = [jnp.sum(jnp.where(sub == ix, before, 0.0), axis=0, keepdims=True).astype(I32)
             for ix in idxs]
    carry[...] = carry[...] + jnp.sum(onehot, axis=1, keepdims=True)
    cnt_ref[...] = carry[...]

    sub8 = lax.broadcasted_iota(I32, (2 * TOPK, tm), 0)
    meta = jnp.zeros((2 * TOPK, tm), I32)
    subw = lax.broadcasted_iota(I32, (LANES, tm), 0)
    wt = jnp.zeros((LANES, tm), F32)
    for k in range(TOPK):
        meta = jnp.where(sub8 == k, idxs[k], meta)
        meta = jnp.where(sub8 == TOPK + k, ranks[k], meta)
        wt = jnp.where(subw == k, ws[k], wt)
    meta_ref[...] = meta
    wts_ref[...] = jnp.transpose(wt)


def _mixer(geom, tm, xs, gb, p, sa, sf, f_p, f_s, mod3, conv_w, conv_b, wco, wfo, wo, g2,
           rw2, rwh, rb, utri):
    tt = geom.tt
    halo = 16
    hb = tm // halo
    nhalo = tt // halo
    row = lambda i: (i, 0)
    full = lambda i: (0, 0)
    x_specs = geom.split_specs(tm, D) if len(xs) == 2 else [pl.BlockSpec((tm, D), row)]
    body = functools.partial(_mixer_body, n_x=len(xs), npt=geom.tp // tm, tm=tm, geom=geom, halo=halo)
    return pl.pallas_call(
        body,
        grid=(tt // tm,),
        in_specs=x_specs + [
            pl.BlockSpec((tm, D), row),
            pl.BlockSpec((tm, D), row),
            pl.BlockSpec((halo, D), lambda i: (jnp.maximum(i * hb - 1, 0), 0)),
            pl.BlockSpec((halo, D), lambda i: (jnp.minimum((i + 1) * hb, nhalo - 1), 0)),
            pl.BlockSpec((tm, D), row),
            pl.BlockSpec((tm, D), row)] + geom.split_specs(tm, DF) + [
            pl.BlockSpec((None, 1, N_MOD * D), lambda i: (geom.seq_of(i * tm), 0, 0)),
            pl.BlockSpec((3, D), full),
            pl.BlockSpec((1, D), full),
            pl.BlockSpec((D, D), full),
            pl.BlockSpec((DF, D), full),
            pl.BlockSpec((D, D), full),
            pl.BlockSpec((1, D), full),
            pl.BlockSpec((D, 2 * LANES), full),
            pl.BlockSpec((D, LANES), full),
            pl.BlockSpec((1, LANES), full),
            pl.BlockSpec((tm, tm), full)],
        out_specs=[pl.BlockSpec((tm, D), row),
                   pl.BlockSpec((tm * ROW_TILE, LANES), row),
                   pl.BlockSpec((2 * TOPK, tm), lambda i: (0, i)),
                   pl.BlockSpec((tm, LANES), row),
                   pl.BlockSpec((E, LANES), full)],
        out_shape=[jax.ShapeDtypeStruct((tt, D), F32),
                   jax.ShapeDtypeStruct((tt * ROW_TILE, LANES), F32),
                   jax.ShapeDtypeStruct((2 * TOPK, tt), I32),
                   jax.ShapeDtypeStruct((tt, LANES), F32),
                   jax.ShapeDtypeStruct((E, LANES), F32)],
        scratch_shapes=[pltpu.VMEM((E, LANES), F32)],
        compiler_params=_params(("arbitrary",)),
        name="mixer_router",
    )(*xs, gb, p, p, p, sa, sf, f_p, f_s, mod3, conv_w, conv_b, wco, wfo, wo, g2, rw2, rwh, rb, utri)


def _row_tile(ref, r):
    return ref.at[pl.ds(pl.multiple_of(r * ROW_TILE, ROW_TILE), ROW_TILE), :]


def _dest_base(g):
    per_chunk = LANES // DMA_GROUP
    chunk = lax.shift_right_logical(g, per_chunk.bit_length() - 1)
    return chunk * (TOPK * LANES) + (g & (per_chunk - 1)) * DMA_GROUP


def _chunk_dest(dest):
    tt = dest.shape[1]
    return dest.reshape(TOPK, tt // LANES, LANES).transpose(1, 0, 2).reshape(-1)


def _dispatch_body(dest_ref, h_ref, xs_hbm, sem, *, tq):
    def issue(g, c):
        base = _dest_base(g)
        for u in range(DMA_GROUP):
            t = g * DMA_GROUP + u
            for k in range(TOPK):
                d = dest_ref[base + (k * LANES + u)]
                pltpu.make_async_copy(_row_tile(h_ref, t), _row_tile(xs_hbm, d),
                                      sem).start(priority=(u * TOPK + k) % 2)
        return c

    lax.fori_loop(0, tq // DMA_GROUP, issue, 0)
    n = tq * TOPK * ROW_TILE
    pltpu.make_async_copy(xs_hbm.at[pl.ds(0, n), :], xs_hbm.at[pl.ds(0, n), :], sem).wait()


def _dispatch(tq, dest, h2r, n_rows):
    tt = h2r.shape[0] // ROW_TILE
    return pl.pallas_call(
        functools.partial(_dispatch_body, tq=tq),
        grid=(tt // tq,),
        in_specs=[pl.BlockSpec((TOPK * tq,), lambda i: (i,), memory_space=pltpu.SMEM),
                  pl.BlockSpec((tq * ROW_TILE, LANES), lambda i: (i, 0))],
        out_specs=pl.BlockSpec(memory_space=pl.ANY),
        out_shape=jax.ShapeDtypeStruct((n_rows * ROW_TILE, LANES), F32),
        scratch_shapes=[pltpu.SemaphoreType.DMA],
        compiler_params=_params(("arbitrary",)),
        name="moe_dispatch",
    )(dest, h2r)


def _ffn_body(be_ref, nv_ref, nu_ref, xs_ref, wgu_ref, bgu_ref, wdn_ref, bdn_ref, os_ref,
              wgu_bf, wdn_bf, *, bm):
    i = pl.program_id(0)

    @pl.when(i < nu_ref[0])
    def _():
        @pl.when((i == 0) | (be_ref[i] != be_ref[jnp.maximum(i - 1, 0)]))
        def _():
            wgu_bf[...] = wgu_ref[...].astype(BF16)
            wdn_bf[...] = wdn_ref[...].astype(BF16)

        x = _from_row_tiles(xs_ref, 0, bm)
        valid = lax.broadcasted_iota(I32, (bm, 1), 0) < nv_ref[i]
        x = jnp.where(valid, x, 0.0).astype(BF16)
        gu = jnp.dot(x, wgu_bf[...], preferred_element_type=F32) + bgu_ref[...]
        glu = jnp.minimum(gu[:, :DFF], LIMIT)
        lin = jnp.clip(gu[:, DFF:], -LIMIT, LIMIT)
        act = glu * jax.nn.sigmoid(ALPHA * glu) * (lin + 1.0)
        out = jnp.dot(act.astype(BF16), wdn_bf[...], preferred_element_type=F32) + bdn_ref[...]
        _to_row_tiles(os_ref, out, bm)


def _expert_ffn(bm, layer, block_expert, n_valid, n_used, xs, wgu, bgu, wdn, bdn):
    n_blocks = xs.shape[0] // (bm * ROW_TILE)
    rows = lambda i, be, nv, nu: (jnp.minimum(i, nu[0] - 1), 0)
    wsel = lambda i, be, nv, nu: (layer, be[i], 0, 0)
    grid_spec = pltpu.PrefetchScalarGridSpec(
        num_scalar_prefetch=3,
        grid=(n_blocks,),
        in_specs=[pl.BlockSpec((bm * ROW_TILE, LANES), rows),
                  pl.BlockSpec((None, None, D, 2 * DFF), wsel),
                  pl.BlockSpec((None, None, 1, 2 * DFF), wsel),
                  pl.BlockSpec((None, None, DFF, D), wsel),
                  pl.BlockSpec((None, None, 1, D), wsel)],
        out_specs=pl.BlockSpec((bm * ROW_TILE, LANES), rows),
        scratch_shapes=[pltpu.VMEM((D, 2 * DFF), BF16), pltpu.VMEM((DFF, D), BF16)],
    )
    return pl.pallas_call(
        functools.partial(_ffn_body, bm=bm),
        grid_spec=grid_spec,
        out_shape=jax.ShapeDtypeStruct(xs.shape, F32),
        compiler_params=_params(("arbitrary",)),
        name="expert_ffn",
    )(block_expert, n_valid, n_used, xs, wgu, bgu, wdn, bdn)


def _combine_body(dcur_ref, dnxt_ref, os_hbm, w_ref, x1_ref, mod_ref, g_ref, *rest, tq, final, npt):
    out_refs, (stage, sems) = rest[:-2], rest[-2:]
    i = pl.program_id(0)
    n = pl.num_programs(0)

    cr = COMBINE_ROWS
    groups_per_chunk = cr // DMA_GROUP
    nrow = TOPK * tq * ROW_TILE

    def issue_group(dref, slot, g):
        base = _dest_base(g)
        row0 = g * (DMA_GROUP * ROW_TILE)
        for u in range(DMA_GROUP):
            for k in range(TOPK):
                r = pl.multiple_of(row0 + (k * tq + u) * ROW_TILE, ROW_TILE)
                pltpu.make_async_copy(_row_tile(os_hbm, dref[base + (k * LANES + u)]),
                                      stage.at[slot, pl.ds(r, ROW_TILE), :],
                                      sems.at[slot]).start(priority=(u * TOPK + k) % 2)

    def wait_slot(slot):
        pltpu.make_async_copy(os_hbm.at[pl.ds(0, nrow), :], stage.at[slot], sems.at[slot]).wait()

    def step(cur, out_ref):
        nxt = 1 - cur
        wait_slot(cur)

        def body(c, carry):
            r0 = pl.multiple_of(c * cr, cr)
            rows = pl.ds(r0, cr)
            tiles = [[stage[cur, pl.ds((k * tq + r0) * ROW_TILE + s, cr, stride=ROW_TILE), :]
                      for s in range(ROW_TILE)] for k in range(TOPK)]
            xv = x1_ref[rows, :]
            wv = w_ref[rows, :]
            for gg in range(groups_per_chunk):
                issue_group(dnxt_ref, nxt, c * groups_per_chunk + gg)
            wk = [jnp.broadcast_to(wv[:, k:k + 1], (cr, LANES)) for k in range(TOPK)]
            x2s = []
            for s in range(ROW_TILE):
                cols = slice(s * LANES, (s + 1) * LANES)
                y = wk[0] * tiles[0][s]
                for k in range(1, TOPK):
                    y = y + wk[k] * tiles[k][s]
                x2s.append(xv[:, cols] + mod_ref[:, 5 * D + s * LANES:5 * D + (s + 1) * LANES] * y)
            if final:
                ssq = x2s[0] * x2s[0]
                for s in range(1, ROW_TILE):
                    ssq = ssq + x2s[s] * x2s[s]
                inv = lax.rsqrt(jnp.sum(ssq, axis=-1, keepdims=True) * (1.0 / D) + EPS)
                x2s = [x2s[s] * inv * g_ref[:, s * LANES:(s + 1) * LANES] for s in range(ROW_TILE)]
            for s in range(ROW_TILE):
                out_ref[rows, s * LANES:(s + 1) * LANES] = x2s[s]
            return carry

        lax.fori_loop(0, tq // cr, body, 0)

        @pl.when(i + 1 == n)
        def _():
            wait_slot(nxt)

    @pl.when(i == 0)
    def _():
        def first(g, carry):
            issue_group(dcur_ref, 0, g)
            return carry

        lax.fori_loop(0, tq // DMA_GROUP, first, 0)

    for cur in range(2):
        if not final:
            @pl.when(i % 2 == cur)
            def _():
                step(cur, out_refs[0])
        else:
            @pl.when((i % 2 == cur) & (i < npt))
            def _():
                step(cur, out_refs[0])

            @pl.when((i % 2 == cur) & (i >= npt))
            def _():
                step(cur, out_refs[1])


def _combine(geom, tq, dest, os_rows, wts, x1, mod3, g_final, final):
    tt = geom.tt
    nt = tt // tq
    if final:
        out_specs = geom.split_specs(tq, D)
        out_shape = [jax.ShapeDtypeStruct((geom.tp, D), F32), jax.ShapeDtypeStruct((geom.ts, D), F32)]
    else:
        out_specs = [pl.BlockSpec((tq, D), lambda i: (i, 0))]
        out_shape = [jax.ShapeDtypeStruct((tt, D), F32)]
    return pl.pallas_call(
        functools.partial(_combine_body, tq=tq, final=final, npt=geom.tp // tq),
        grid=(nt,),
        in_specs=[pl.BlockSpec((TOPK * tq,), lambda i: (i,), memory_space=pltpu.SMEM),
                  pl.BlockSpec((TOPK * tq,), lambda i: (jnp.minimum(i + 1, nt - 1),),
                               memory_space=pltpu.SMEM),
                  pl.BlockSpec(memory_space=pl.ANY),
                  pl.BlockSpec((tq, LANES), lambda i: (i, 0)),
                  pl.BlockSpec((tq, D), lambda i: (i, 0)),
                  pl.BlockSpec((None, 1, N_MOD * D), lambda i: (geom.seq_of(i * tq), 0, 0)),
                  pl.BlockSpec((1, D), lambda i: (0, 0))],
        out_specs=out_specs,
        out_shape=out_shape,
        scratch_shapes=[pltpu.VMEM((2, TOPK * tq * ROW_TILE, LANES), F32),
                        pltpu.SemaphoreType.DMA((2,))],
        compiler_params=_params(("arbitrary",)),
        name="moe_combine",
    )(dest, dest, os_rows, wts, x1, mod3, g_final)


def _route_plan(meta_t, counts_f, bm, n_blocks):
    idx = meta_t[0:TOPK]
    rank = meta_t[TOPK:2 * TOPK]
    counts = counts_f[:, 0].astype(I32)
    padded = (counts + bm - 1) // bm * bm
    pad_end = jnp.cumsum(padded)
    pad_start = pad_end - padded
    dest = rank
    for e in range(E):
        dest = dest + jnp.where(idx == e, pad_start[e], 0)
    n_used = (pad_end[E - 1] // bm).astype(I32)
    blk = jnp.arange(n_blocks, dtype=I32)
    be = jnp.minimum(jnp.sum((pad_end[None, :] <= (blk * bm)[:, None]).astype(I32), axis=1), E - 1)
    be = jnp.where(blk < n_used, be, be[n_used - 1]).astype(I32)
    n_valid = jnp.clip((pad_start + counts)[be] - blk * bm, 0, bm).astype(I32)
    return _chunk_dest(dest), be, n_valid, n_used.reshape(1)


def _trunk(geom, x_p, x_s, c_all, mod_w, mod_b, norm1_g, norm2_g, w_in, conv_w, conv_b, w_conv_out,
           w_fourier_out, w_o, router_w, router_b, w_gate_up, b_gate_up, w_down, b_down, final_g,
           tm=512, tq_dispatch=2048, tq_combine=512, bm=512):
    depth = mod_w.shape[0]
    tt = geom.tt
    n_blocks = -(-tt * TOPK // bm) + E
    n_rows = n_blocks * bm
    cat = jnp.asarray(_chan_dft(), BF16)
    utri = jnp.asarray(np.triu(np.ones((tm, tm), np.float32), 1), BF16)
    c_pad = jnp.zeros((-(-geom.nseq // 8) * 8, D), F32).at[:geom.nseq].set(c_all)
    xs = (x_p, x_s)
    for l in range(depth):
        mod = _modulation(c_pad, mod_w[l], mod_b[l])
        mod3 = mod.reshape(mod.shape[0], 1, N_MOD * D)
        gb, p, sa, sf, a_p, b_p, a_s, b_s = _inproj(geom, tm, xs, mod3, norm1_g[l].reshape(1, D),
                                                    w_in[l].astype(BF16), cat)
        f_p = _fourier(a_p, b_p, geom.bp, geom.sp)
        f_s = _fourier(a_s, b_s, geom.bs, geom.ss)
        rw = jnp.zeros((D, LANES), F32).at[:, :E].set(router_w[l])
        rwh = rw.astype(BF16)
        rw2 = jnp.concatenate([rwh, (rw - rwh.astype(F32)).astype(BF16)], axis=1)
        rb = jnp.full((1, LANES), NEG, F32).at[0, :E].set(router_b[l])
        x1, h2r, meta_t, wts, counts = _mixer(
            geom, tm, xs, gb, p, sa, sf, f_p, f_s, mod3, conv_w[l], conv_b[l].reshape(1, D),
            w_conv_out[l].astype(BF16), w_fourier_out[l].astype(BF16), w_o[l].astype(BF16),
            norm2_g[l].reshape(1, D), rw2, rwh, rb, utri)
        dest, block_expert, n_valid, n_used = _route_plan(meta_t, counts, bm, n_blocks)
        xrows = _dispatch(tq_dispatch, dest, h2r, n_rows)
        os_rows = _expert_ffn(bm, l, block_expert, n_valid, n_used, xrows,
                              w_gate_up, b_gate_up.reshape(depth, E, 1, 2 * DFF),
                              w_down, b_down.reshape(depth, E, 1, D))
        xs = tuple(_combine(geom, tq_combine, dest, os_rows, wts, x1, mod3, final_g.reshape(1, D),
                            final=(l == depth - 1)))
    return xs


def kernel(x_prompt, x_sample, c_prompt, c_sample, mod_w, mod_b, norm1_g, norm2_g, w_in, conv_w,
           conv_b, w_conv_out, w_fourier_out, w_o, router_w, router_b, w_gate_up, b_gate_up,
           w_down, b_down, final_g):
    bp, sp, _ = x_prompt.shape
    bs, ss, _ = x_sample.shape
    geom = _Geom(bp, sp, bs, ss)
    c_all = jnp.concatenate([c_prompt, c_sample], axis=0)
    y_p, y_s = _trunk(geom, x_prompt.reshape(bp * sp, D), x_sample.reshape(bs * ss, D), c_all,
                      mod_w, mod_b, norm1_g, norm2_g, w_in, conv_w, conv_b, w_conv_out,
                      w_fourier_out, w_o, router_w, router_b, w_gate_up, b_gate_up, w_down, b_down,
                      final_g)
    return (y_p.reshape(bp, sp, D), y_s.reshape(bs, ss, D))
```

```python
import functools
import math

import numpy as np
import jax
import jax.numpy as jnp
from jax import lax
from jax.experimental import pallas as pl
from jax.experimental.pallas import tpu as pltpu

F32 = jnp.float32
BF16 = jnp.bfloat16
I32 = jnp.int32

D = 1024
DF = 512
NG = 4
DG = DF // NG
E = 32
TOPK = 4
DFF = 1024
N_MOD = 6
ALPHA = 1.702
LIMIT = 7.0
EPS = 1e-5
LANES = 128
SUBLANES = 8
ROW_TILE = D // LANES
FFT_N1 = 128
FFT_K1_TILE = SUBLANES
VMEM_LIMIT = 56 * 1024 * 1024
NEG = -1e30
DMA_GROUP = 8
assert ROW_TILE == SUBLANES


def _params(sem, vmem=VMEM_LIMIT):
    return pltpu.CompilerParams(dimension_semantics=sem, vmem_limit_bytes=vmem)


class _Geom:
    def __init__(self, bp, sp, bs, ss):
        assert bp > 0 and bs > 0
        self.bp, self.sp, self.bs, self.ss = bp, sp, bs, ss
        self.tp = bp * sp
        self.ts = bs * ss
        self.tt = self.tp + self.ts
        self.nseq = bp + bs

    def seq_of(self, t0):
        return jnp.where(t0 < self.tp, t0 // self.sp, self.bp + (t0 - self.tp) // self.ss)

    def offset_len(self, t0):
        in_p = t0 < self.tp
        off = jnp.where(in_p, t0 % self.sp, (t0 - self.tp) % self.ss)
        slen = jnp.where(in_p, self.sp, self.ss)
        return off, slen

    def split_specs(self, tm, width):
        npt = self.tp // tm
        return [pl.BlockSpec((tm, width), lambda i: (jnp.minimum(i, npt - 1), 0)),
                pl.BlockSpec((tm, width), lambda i: (jnp.maximum(i - npt, 0), 0))]


def _to_row_tiles(ref, val, rows):
    for s in range(ROW_TILE):
        ref[pl.ds(s, rows, stride=ROW_TILE), :] = val[:, s * LANES:(s + 1) * LANES]


def _from_row_tiles(ref, start, rows):
    return jnp.concatenate(
        [ref[pl.ds(start + s, rows, stride=ROW_TILE), :] for s in range(ROW_TILE)], axis=1)


def _mod_body(c_ref, w_ref, b_ref, o_ref):
    c = c_ref[...]
    s = c * jax.nn.sigmoid(c)
    o_ref[...] = jnp.dot(s, w_ref[...], preferred_element_type=F32,
                         precision=lax.Precision.HIGHEST) + b_ref[...]


def _modulation(c_pad, mod_w, mod_b):
    rows = c_pad.shape[0]
    return pl.pallas_call(
        _mod_body,
        grid=(N_MOD,),
        in_specs=[pl.BlockSpec((rows, D), lambda j: (0, 0)),
                  pl.BlockSpec((D, D), lambda j: (0, j)),
                  pl.BlockSpec((1, D), lambda j: (0, j))],
        out_specs=pl.BlockSpec((rows, D), lambda j: (0, j)),
        out_shape=jax.ShapeDtypeStruct((rows, N_MOD * D), F32),
        compiler_params=_params(("parallel",)),
        name="modulation",
    )(c_pad, mod_w, mod_b.reshape(1, N_MOD * D))


def _inproj_body(*refs, n_x, npt):
    x_refs = refs[:n_x]
    (mod_ref, g_ref, w_ref, cat_ref,
     gb_ref, p_ref, sa_ref, sf_ref, ap_ref, bp_ref, as_ref, bs_ref) = refs[n_x:]
    is_p = pl.program_id(0) < npt
    x = x_refs[0][...] if n_x == 1 else jnp.where(is_p, x_refs[0][...], x_refs[1][...])
    ms = jnp.mean(x * x, axis=-1, keepdims=True)
    xn = x * lax.rsqrt(ms + EPS) * g_ref[...]
    shift = mod_ref[:, 0:D]
    scale = mod_ref[:, D:2 * D]
    h = (xn * (1.0 + scale) + shift).astype(BF16)

    def proj(lo, hi):
        return jnp.dot(h, w_ref[:, lo:hi], preferred_element_type=F32)

    gb_ref[...] = proj(0, D).astype(BF16)
    p_ref[...] = (proj(D, 2 * D) * proj(2 * D, 3 * D)).astype(BF16)
    uf = proj(3 * D, 3 * D + DF).astype(BF16)
    abs_ = [jnp.dot(uf[:, g * DG:(g + 1) * DG], cat_ref[...], preferred_element_type=F32)
            for g in range(NG)]

    def put_ab(a_ref, b_ref):
        for g in range(NG):
            a_ref[g] = abs_[g][:, :DG]
            b_ref[g] = abs_[g][:, DG:]

    @pl.when(is_p)
    def _():
        put_ab(ap_ref, bp_ref)

    @pl.when(jnp.logical_not(is_p))
    def _():
        put_ab(as_ref, bs_ref)

    sa_ref[...] = jax.nn.sigmoid(proj(3 * D + DF, 4 * D + DF)).astype(BF16)
    sf_ref[...] = jax.nn.sigmoid(proj(4 * D + DF, 5 * D + DF)).astype(BF16)


def _inproj(geom, tm, xs, mod3, g1, w_in_bf, cat):
    tt = geom.tt
    npt = geom.tp // tm
    row = lambda i: (i, 0)
    full = lambda i: (0, 0)
    bf = lambda r, n: jax.ShapeDtypeStruct((r, n), BF16)
    x_specs = geom.split_specs(tm, D) if len(xs) == 2 else [pl.BlockSpec((tm, D), row)]
    ab_specs = [pl.BlockSpec((NG, tm, DG), lambda i: (0, jnp.minimum(i, npt - 1), 0)),
                pl.BlockSpec((NG, tm, DG), lambda i: (0, jnp.maximum(i - npt, 0), 0))]
    ab_shape = lambda r: jax.ShapeDtypeStruct((NG, r, DG), F32)
    return pl.pallas_call(
        functools.partial(_inproj_body, n_x=len(xs), npt=npt),
        grid=(tt // tm,),
        in_specs=x_specs + [
            pl.BlockSpec((None, 1, N_MOD * D), lambda i: (geom.seq_of(i * tm), 0, 0)),
            pl.BlockSpec((1, D), full),
            pl.BlockSpec(w_in_bf.shape, full),
            pl.BlockSpec(cat.shape, full)],
        out_specs=[pl.BlockSpec((tm, D), row), pl.BlockSpec((tm, D), row),
                   pl.BlockSpec((tm, D), row), pl.BlockSpec((tm, D), row),
                   ab_specs[0], ab_specs[0], ab_specs[1], ab_specs[1]],
        out_shape=[bf(tt, D), bf(tt, D), bf(tt, D), bf(tt, D),
                   ab_shape(geom.tp), ab_shape(geom.tp), ab_shape(geom.ts), ab_shape(geom.ts)],
        compiler_params=_params(("arbitrary",)),
        name="inproj",
    )(*xs, mod3, g1, w_in_bf, cat)


@functools.lru_cache(maxsize=None)
def _fft_consts(s):
    n1 = FFT_N1
    n2 = s // n1
    a = np.arange(n1)
    ang = 2.0 * np.pi * (np.outer(a, a) % n1) / n1
    c1, s1 = np.cos(ang), np.sin(ang)
    w1 = np.block([[c1, s1], [-s1, c1]])
    ang = 2.0 * np.pi * (np.outer(np.arange(n1), np.arange(n2)) % s) / s
    tc = np.repeat(np.cos(ang), LANES, axis=1)
    ts = np.repeat(np.sin(ang), LANES, axis=1)
    m = np.arange(n2)
    ang = 2.0 * np.pi * (np.outer(m, m) % n2) / n2
    c2, s2 = np.cos(ang), np.sin(ang)
    kt = FFT_K1_TILE
    wexp = np.zeros((n2, kt, 2, kt, n2))
    for j in range(kt):
        wexp[:, j, 0, j, :] = c2
        wexp[:, j, 1, j, :] = s2
    wexp = wexp.reshape(n2 * kt, 2 * kt * n2)
    return (np.asarray(w1, np.float32), np.asarray(tc, np.float32), np.asarray(ts, np.float32),
            np.asarray(wexp, np.float32))


@functools.lru_cache(maxsize=None)
def _chan_dft():
    a = np.arange(DG)
    ang = 2.0 * np.pi * (np.outer(a, a) % DG) / DG
    return np.asarray(np.concatenate([np.cos(ang), -np.sin(ang)], axis=1), np.float32)


def _fft1_body(a_ref, b_ref, w_ref, tc_ref, ts_ref, yr_ref, yi_ref, abuf, bbuf, yrbuf, yibuf, *, tn2):
    n1 = FFT_N1
    for g in range(NG):
        abuf[g] = a_ref[g].reshape(n1 * tn2, LANES)
        bbuf[g] = b_ref[g].reshape(n1 * tn2, LANES)

    for j in range(tn2):
        sl = pl.ds(j, n1, stride=tn2)
        a = jnp.concatenate([abuf[g, sl, :] for g in range(NG)], axis=1)
        b = jnp.concatenate([bbuf[g, sl, :] for g in range(NG)], axis=1)
        ab = jnp.concatenate([a, b], axis=0).astype(BF16)
        y = jnp.dot(w_ref[...], ab, preferred_element_type=F32)
        c = tc_ref[:, j * LANES:(j + 1) * LANES]
        s = ts_ref[:, j * LANES:(j + 1) * LANES]
        for g in range(NG):
            yr = y[:n1, g * LANES:(g + 1) * LANES]
            yi = y[n1:, g * LANES:(g + 1) * LANES]
            yrbuf[g, sl, :] = yr * c + yi * s
            yibuf[g, sl, :] = yi * c - yr * s

    for g in range(NG):
        yr_ref[g] = yrbuf[g].reshape(n1, tn2, LANES)
        yi_ref[g] = yibuf[g].reshape(n1, tn2, LANES)


def _fft2_body(yr_ref, yi_ref, w_ref, f_ref, *, n2, scale):
    yr = jnp.concatenate([yr_ref[g] for g in range(NG)], axis=1)
    yi = jnp.concatenate([yi_ref[g] for g in range(NG)], axis=1)
    rhs = jnp.concatenate([yr, yi], axis=0).astype(BF16)
    z = jnp.dot(w_ref[...], rhs, preferred_element_type=F32) * scale
    f_ref[...] = z.reshape(n2, FFT_K1_TILE, DF)


def _fourier(a, b, nseq, s):
    n1 = FFT_N1
    n2 = s // n1
    tn2 = SUBLANES
    kt = FFT_K1_TILE
    w1, tc, ts, wexp = _fft_consts(s)
    a5 = a.reshape(NG, nseq, n1, n2, DG)
    b5 = b.reshape(NG, nseq, n1, n2, DG)
    blk = pl.BlockSpec((NG, None, n1, tn2, DG), lambda q, j: (0, q, 0, j, 0))
    yr, yi = pl.pallas_call(
        functools.partial(_fft1_body, tn2=tn2),
        grid=(nseq, n2 // tn2),
        in_specs=[blk, blk,
                  pl.BlockSpec((2 * n1, 2 * n1), lambda q, j: (0, 0)),
                  pl.BlockSpec((n1, tn2 * LANES), lambda q, j: (0, j)),
                  pl.BlockSpec((n1, tn2 * LANES), lambda q, j: (0, j))],
        out_specs=[blk, blk],
        out_shape=[jax.ShapeDtypeStruct((NG, nseq, n1, n2, DG), F32)] * 2,
        scratch_shapes=[pltpu.VMEM((NG, n1 * tn2, DG), F32)] * 4,
        compiler_params=_params(("parallel", "parallel")),
        name="fft_stage1",
    )(a5, b5, jnp.asarray(w1, BF16), jnp.asarray(tc), jnp.asarray(ts))
    yr = yr.reshape(NG, nseq, n1 * n2, DG)
    yi = yi.reshape(NG, nseq, n1 * n2, DG)
    yblk = pl.BlockSpec((NG, None, kt * n2, DG), lambda q, j: (0, q, j, 0))
    f = pl.pallas_call(
        functools.partial(_fft2_body, n2=n2, scale=1.0 / math.sqrt(s * DG)),
        grid=(nseq, n1 // kt),
        in_specs=[yblk, yblk,
                  pl.BlockSpec(wexp.shape, lambda q, j: (0, 0))],
        out_specs=pl.BlockSpec((None, n2, kt, DF), lambda q, j: (q, 0, j, 0)),
        out_shape=jax.ShapeDtypeStruct((nseq, n2, n1, DF), F32),
        compiler_params=_params(("parallel", "parallel")),
        name="fft_stage2",
    )(yr, yi, jnp.asarray(wexp, BF16))
    return f.reshape(nseq * s, DF)


def _mixer_body(*refs, n_x, npt, tm, geom, halo):
    x_refs = refs[:n_x]
    (gb_ref, p_ref, pprev_ref, pnext_ref, sa_ref, sf_ref, fp_ref, fs_ref, mod_ref,
     cw_ref, cb_ref, wco_ref, wfo_ref, wo_ref, g2_ref, rw2_ref, rwh_ref, rb_ref, utri_ref,
     x1_ref, h2_ref, meta_ref, wts_ref, cnt_ref, carry) = refs[n_x:]
    i = pl.program_id(0)
    is_p = i < npt

    @pl.when(i == 0)
    def _():
        carry[...] = jnp.zeros_like(carry)

    x = x_refs[0][...] if n_x == 1 else jnp.where(is_p, x_refs[0][...], x_refs[1][...])
    f = jnp.where(is_p, fp_ref[...], fs_ref[...])

    off, slen = geom.offset_len(i * tm)
    keep_prev = jnp.where(off == 0, 0.0, 1.0)
    keep_next = jnp.where(off + tm == slen, 0.0, 1.0)
    p = p_ref[...].astype(F32)
    prev_row = pprev_ref[halo - 1:halo, :].astype(F32) * keep_prev
    next_row = pnext_ref[0:1, :].astype(F32) * keep_next
    row = lax.broadcasted_iota(I32, (tm, 1), 0)
    pm = jnp.where(row == 0, prev_row, pltpu.roll(p, 1, 0))
    pp = jnp.where(row == tm - 1, next_row, pltpu.roll(p, tm - 1, 0))
    conv = cw_ref[0:1, :] * pm + cw_ref[1:2, :] * p + cw_ref[2:3, :] * pp + cb_ref[...]
    ya_in = (gb_ref[...].astype(F32) * conv).astype(BF16)
    ya = jnp.dot(ya_in, wco_ref[...], preferred_element_type=F32)
    yf = jnp.dot(f.astype(BF16), wfo_ref[...], preferred_element_type=F32)
    merged = sa_ref[...].astype(F32) * ya + sf_ref[...].astype(F32) * yf
    mix = jnp.dot(merged.astype(BF16), wo_ref[...], preferred_element_type=F32)
    x1 = x + mod_ref[:, 2 * D:3 * D] * mix
    x1_ref[...] = x1

    ms = jnp.mean(x1 * x1, axis=-1, keepdims=True)
    xn = x1 * lax.rsqrt(ms + EPS) * g2_ref[...]
    h2 = xn * (1.0 + mod_ref[:, 4 * D:5 * D]) + mod_ref[:, 3 * D:4 * D]
    _to_row_tiles(h2_ref, h2, tm)

    hh = h2.astype(BF16)
    hl = (h2 - hh.astype(F32)).astype(BF16)
    two = jnp.dot(hh, rw2_ref[...], preferred_element_type=F32)
    logits = (two[:, :LANES] + two[:, LANES:]
              + jnp.dot(hl, rwh_ref[...], preferred_element_type=F32)) + rb_ref[...]

    l = jnp.transpose(logits)[0:E, :]
    sub = lax.broadcasted_iota(I32, (E, tm), 0)
    vals, idxs = [], []
    free = jnp.ones((E, tm), jnp.bool_)
    for _ in range(TOPK):
        m = jnp.max(jnp.where(free, l, -jnp.inf), axis=0, keepdims=True)
        ix = jnp.min(jnp.where(free & (l == m), sub, E), axis=0, keepdims=True)
        vals.append(m)
        idxs.append(ix)
        free = free & (sub != ix)
    exps = [jnp.exp(v - vals[0]) for v in vals]
    denom = exps[0] + exps[1] + exps[2] + exps[3]
    ws = [e / denom for e in exps]

    onehot = jnp.zeros((E, tm), F32)
    for ix in idxs:
        onehot = onehot + jnp.where(sub == ix, 1.0, 0.0)
    before = jnp.dot(onehot.astype(BF16), utri_ref[...], preferred_element_type=F32) + carry[:, 0:1]
    ranks = [jnp.sum(jnp.where(sub == ix, before, 0.0), axis=0, keepdims=True).astype(I32)
             for ix in idxs]
    carry[...] = carry[...] + jnp.sum(onehot, axis=1, keepdims=True)
    cnt_ref[...] = carry[...]

    sub8 = lax.broadcasted_iota(I32, (2 * TOPK, tm), 0)
    meta = jnp.zeros((2 * TOPK, tm), I32)
    subw = lax.broadcasted_iota(I32, (LANES, tm), 0)
    wt = jnp.zeros((LANES, tm), F32)
    for k in range(TOPK):
        meta = jnp.where(sub8 == k, idxs[k], meta)
        meta = jnp.where(sub8 == TOPK + k, ranks[k], meta)
        wt = jnp.where(subw == k, ws[k], wt)
    meta_ref[...] = meta
    wts_ref[...] = jnp.transpose(wt)


def _mixer(geom, tm, xs, gb, p, sa, sf, f_p, f_s, mod3, conv_w, conv_b, wco, wfo, wo, g2,
           rw2, rwh, rb, utri):
    tt = geom.tt
    halo = 16
    hb = tm // halo
    nhalo = tt // halo
    row = lambda i: (i, 0)
    full = lambda i: (0, 0)
    x_specs = geom.split_specs(tm, D) if len(xs) == 2 else [pl.BlockSpec((tm, D), row)]
    body = functools.partial(_mixer_body, n_x=len(xs), npt=geom.tp // tm, tm=tm, geom=geom, halo=halo)
    return pl.pallas_call(
        body,
        grid=(tt // tm,),
        in_specs=x_specs + [
            pl.BlockSpec((tm, D), row),
            pl.BlockSpec((tm, D), row),
            pl.BlockSpec((halo, D), lambda i: (jnp.maximum(i * hb - 1, 0), 0)),
            pl.BlockSpec((halo, D), lambda i: (jnp.minimum((i + 1) * hb, nhalo - 1), 0)),
            pl.BlockSpec((tm, D), row),
            pl.BlockSpec((tm, D), row)] + geom.split_specs(tm, DF) + [
            pl.BlockSpec((None, 1, N_MOD * D), lambda i: (geom.seq_of(i * tm), 0, 0)),
            pl.BlockSpec((3, D), full),
            pl.BlockSpec((1, D), full),
            pl.BlockSpec((D, D), full),
            pl.BlockSpec((DF, D), full),
            pl.BlockSpec((D, D), full),
            pl.BlockSpec((1, D), full),
            pl.BlockSpec((D, 2 * LANES), full),
            pl.BlockSpec((D, LANES), full),
            pl.BlockSpec((1, LANES), full),
            pl.BlockSpec((tm, tm), full)],
        out_specs=[pl.BlockSpec((tm, D), row),
                   pl.BlockSpec((tm * ROW_TILE, LANES), row),
                   pl.BlockSpec((2 * TOPK, tm), lambda i: (0, i)),
                   pl.BlockSpec((tm, LANES), row),
                   pl.BlockSpec((E, LANES), full)],
        out_shape=[jax.ShapeDtypeStruct((tt, D), F32),
                   jax.ShapeDtypeStruct((tt * ROW_TILE, LANES), F32),
                   jax.ShapeDtypeStruct((2 * TOPK, tt), I32),
                   jax.ShapeDtypeStruct((tt, LANES), F32),
                   jax.ShapeDtypeStruct((E, LANES), F32)],
        scratch_shapes=[pltpu.VMEM((E, LANES), F32)],
        compiler_params=_params(("arbitrary",)),
        name="mixer_router",
    )(*xs, gb, p, p, p, sa, sf, f_p, f_s, mod3, conv_w, conv_b, wco, wfo, wo, g2, rw2, rwh, rb, utri)


def _row_tile(ref, r):
    return ref.at[pl.ds(pl.multiple_of(r * ROW_TILE, ROW_TILE), ROW_TILE), :]


def _dest_base(g):
    per_chunk = LANES // DMA_GROUP
    chunk = lax.shift_right_logical(g, per_chunk.bit_length() - 1)
    return chunk * (TOPK * LANES) + (g & (per_chunk - 1)) * DMA_GROUP


def _chunk_dest(dest):
    tt = dest.shape[1]
    return dest.reshape(TOPK, tt // LANES, LANES).transpose(1, 0, 2).reshape(-1)


def _dispatch_body(dest_ref, h_ref, xs_hbm, sem, *, tq):
    def issue(g, c):
        base = _dest_base(g)
        for u in range(DMA_GROUP):
            t = g * DMA_GROUP + u
            for k in range(TOPK):
                d = dest_ref[base + (k * LANES + u)]
                pltpu.make_async_copy(_row_tile(h_ref, t), _row_tile(xs_hbm, d),
                                      sem).start(priority=(u * TOPK + k) % 2)
        return c

    lax.fori_loop(0, tq // DMA_GROUP, issue, 0)
    n = tq * TOPK * ROW_TILE
    pltpu.make_async_copy(xs_hbm.at[pl.ds(0, n), :], xs_hbm.at[pl.ds(0, n), :], sem).wait()


def _dispatch(tq, dest, h2r, n_rows):
    tt = h2r.shape[0] // ROW_TILE
    return pl.pallas_call(
        functools.partial(_dispatch_body, tq=tq),
        grid=(tt // tq,),
        in_specs=[pl.BlockSpec((TOPK * tq,), lambda i: (i,), memory_space=pltpu.SMEM),
                  pl.BlockSpec((tq * ROW_TILE, LANES), lambda i: (i, 0))],
        out_specs=pl.BlockSpec(memory_space=pl.ANY),
        out_shape=jax.ShapeDtypeStruct((n_rows * ROW_TILE, LANES), F32),
        scratch_shapes=[pltpu.SemaphoreType.DMA],
        compiler_params=_params(("arbitrary",)),
        name="moe_dispatch",
    )(dest, h2r)


def _ffn_body(be_ref, nv_ref, nu_ref, xs_ref, wgu_ref, bgu_ref, wdn_ref, bdn_ref, os_ref,
              wgu_bf, wdn_bf, *, bm):
    i = pl.program_id(0)

    @pl.when(i < nu_ref[0])
    def _():
        @pl.when((i == 0) | (be_ref[i] != be_ref[jnp.maximum(i - 1, 0)]))
        def _():
            wgu_bf[...] = wgu_ref[...].astype(BF16)
            wdn_bf[...] = wdn_ref[...].astype(BF16)

        x = _from_row_tiles(xs_ref, 0, bm)
        valid = lax.broadcasted_iota(I32, (bm, 1), 0) < nv_ref[i]
        x = jnp.where(valid, x, 0.0).astype(BF16)
        gu = jnp.dot(x, wgu_bf[...], preferred_element_type=F32) + bgu_ref[...]
        glu = jnp.minimum(gu[:, :DFF], LIMIT)
        lin = jnp.clip(gu[:, DFF:], -LIMIT, LIMIT)
        act = glu * jax.nn.sigmoid(ALPHA * glu) * (lin + 1.0)
        out = jnp.dot(act.astype(BF16), wdn_bf[...], preferred_element_type=F32) + bdn_ref[...]
        _to_row_tiles(os_ref, out, bm)


def _expert_ffn(bm, layer, block_expert, n_valid, n_used, xs, wgu, bgu, wdn, bdn):
    n_blocks = xs.shape[0] // (bm * ROW_TILE)
    rows = lambda i, be, nv, nu: (jnp.minimum(i, nu[0] - 1), 0)
    wsel = lambda i, be, nv, nu: (layer, be[i], 0, 0)
    grid_spec = pltpu.PrefetchScalarGridSpec(
        num_scalar_prefetch=3,
        grid=(n_blocks,),
        in_specs=[pl.BlockSpec((bm * ROW_TILE, LANES), rows),
                  pl.BlockSpec((None, None, D, 2 * DFF), wsel),
                  pl.BlockSpec((None, None, 1, 2 * DFF), wsel),
                  pl.BlockSpec((None, None, DFF, D), wsel),
                  pl.BlockSpec((None, None, 1, D), wsel)],
        out_specs=pl.BlockSpec((bm * ROW_TILE, LANES), rows),
        scratch_shapes=[pltpu.VMEM((D, 2 * DFF), BF16), pltpu.VMEM((DFF, D), BF16)],
    )
    return pl.pallas_call(
        functools.partial(_ffn_body, bm=bm),
        grid_spec=grid_spec,
        out_shape=jax.ShapeDtypeStruct(xs.shape, F32),
        compiler_params=_params(("arbitrary",)),
        name="expert_ffn",
    )(block_expert, n_valid, n_used, xs, wgu, bgu, wdn, bdn)


def _combine_body(dcur_ref, dnxt_ref, os_hbm, w_ref, x1_ref, mod_ref, g_ref, *rest, tq, final, npt):
    out_refs, (stage, sems) = rest[:-2], rest[-2:]
    i = pl.program_id(0)
    n = pl.num_programs(0)

    def issue(dref, slot):
        def grp(g, c):
            base = _dest_base(g)
            for u in range(DMA_GROUP):
                t = g * DMA_GROUP + u
                for k in range(TOPK):
                    dst = stage.at[slot, pl.ds(pl.multiple_of((k * tq + t) * ROW_TILE, ROW_TILE), ROW_TILE), :]
                    pltpu.make_async_copy(_row_tile(os_hbm, dref[base + (k * LANES + u)]), dst,
                                          sems.at[slot]).start(priority=(u * TOPK + k) % 2)
            return c
        lax.fori_loop(0, tq // DMA_GROUP, grp, 0)

    @pl.when(i == 0)
    def _():
        issue(dcur_ref, 0)

    @pl.when(i + 1 < n)
    def _():
        issue(dnxt_ref, (i + 1) % 2)

    slot = i % 2
    nrow = TOPK * tq * ROW_TILE
    pltpu.make_async_copy(os_hbm.at[pl.ds(0, nrow), :], stage.at[slot], sems.at[slot]).wait()
    buf = stage.at[slot]
    y = w_ref[:, 0:1] * _from_row_tiles(buf, 0, tq)
    for k in range(1, TOPK):
        y = y + w_ref[:, k:k + 1] * _from_row_tiles(buf, k * tq * ROW_TILE, tq)
    x2 = x1_ref[...] + mod_ref[:, 5 * D:6 * D] * y
    if not final:
        out_refs[0][...] = x2
    else:
        ms = jnp.mean(x2 * x2, axis=-1, keepdims=True)
        x2 = x2 * lax.rsqrt(ms + EPS) * g_ref[...]

        @pl.when(i < npt)
        def _():
            out_refs[0][...] = x2

        @pl.when(i >= npt)
        def _():
            out_refs[1][...] = x2


def _combine(geom, tq, dest, os_rows, wts, x1, mod3, g_final, final):
    tt = geom.tt
    nt = tt // tq
    if final:
        out_specs = geom.split_specs(tq, D)
        out_shape = [jax.ShapeDtypeStruct((geom.tp, D), F32), jax.ShapeDtypeStruct((geom.ts, D), F32)]
    else:
        out_specs = [pl.BlockSpec((tq, D), lambda i: (i, 0))]
        out_shape = [jax.ShapeDtypeStruct((tt, D), F32)]
    return pl.pallas_call(
        functools.partial(_combine_body, tq=tq, final=final, npt=geom.tp // tq),
        grid=(nt,),
        in_specs=[pl.BlockSpec((TOPK * tq,), lambda i: (i,), memory_space=pltpu.SMEM),
                  pl.BlockSpec((TOPK * tq,), lambda i: (jnp.minimum(i + 1, nt - 1),),
                               memory_space=pltpu.SMEM),
                  pl.BlockSpec(memory_space=pl.ANY),
                  pl.BlockSpec((tq, LANES), lambda i: (i, 0)),
                  pl.BlockSpec((tq, D), lambda i: (i, 0)),
                  pl.BlockSpec((None, 1, N_MOD * D), lambda i: (geom.seq_of(i * tq), 0, 0)),
                  pl.BlockSpec((1, D), lambda i: (0, 0))],
        out_specs=out_specs,
        out_shape=out_shape,
        scratch_shapes=[pltpu.VMEM((2, TOPK * tq * ROW_TILE, LANES), F32),
                        pltpu.SemaphoreType.DMA((2,))],
        compiler_params=_params(("arbitrary",)),
        name="moe_combine",
    )(dest, dest, os_rows, wts, x1, mod3, g_final)


def _route_plan(meta_t, counts_f, bm, n_blocks):
    idx = meta_t[0:TOPK]
    rank = meta_t[TOPK:2 * TOPK]
    counts = counts_f[:, 0].astype(I32)
    padded = (counts + bm - 1) // bm * bm
    pad_end = jnp.cumsum(padded)
    pad_start = pad_end - padded
    dest = rank
    for e in range(E):
        dest = dest + jnp.where(idx == e, pad_start[e], 0)
    n_used = (pad_end[E - 1] // bm).astype(I32)
    blk = jnp.arange(n_blocks, dtype=I32)
    be = jnp.minimum(jnp.sum((pad_end[None, :] <= (blk * bm)[:, None]).astype(I32), axis=1), E - 1)
    be = jnp.where(blk < n_used, be, be[n_used - 1]).astype(I32)
    n_valid = jnp.clip((pad_start + counts)[be] - blk * bm, 0, bm).astype(I32)
    return _chunk_dest(dest), be, n_valid, n_used.reshape(1)


def _trunk(geom, x_p, x_s, c_all, mod_w, mod_b, norm1_g, norm2_g, w_in, conv_w, conv_b, w_conv_out,
           w_fourier_out, w_o, router_w, router_b, w_gate_up, b_gate_up, w_down, b_down, final_g,
           tm=512, tq_dispatch=2048, tq_combine=512, bm=512):
    depth = mod_w.shape[0]
    tt = geom.tt
    n_blocks = -(-tt * TOPK // bm) + E
    n_rows = n_blocks * bm
    cat = jnp.asarray(_chan_dft(), BF16)
    utri = jnp.asarray(np.triu(np.ones((tm, tm), np.float32), 1), BF16)
    c_pad = jnp.zeros((-(-geom.nseq // 8) * 8, D), F32).at[:geom.nseq].set(c_all)
    xs = (x_p, x_s)
    for l in range(depth):
        mod = _modulation(c_pad, mod_w[l], mod_b[l])
        mod3 = mod.reshape(mod.shape[0], 1, N_MOD * D)
        gb, p, sa, sf, a_p, b_p, a_s, b_s = _inproj(geom, tm, xs, mod3, norm1_g[l].reshape(1, D),
                                                    w_in[l].astype(BF16), cat)
        f_p = _fourier(a_p, b_p, geom.bp, geom.sp)
        f_s = _fourier(a_s, b_s, geom.bs, geom.ss)
        rw = jnp.zeros((D, LANES), F32).at[:, :E].set(router_w[l])
        rwh = rw.astype(BF16)
        rw2 = jnp.concatenate([rwh, (rw - rwh.astype(F32)).astype(BF16)], axis=1)
        rb = jnp.full((1, LANES), NEG, F32).at[0, :E].set(router_b[l])
        x1, h2r, meta_t, wts, counts = _mixer(
            geom, tm, xs, gb, p, sa, sf, f_p, f_s, mod3, conv_w[l], conv_b[l].reshape(1, D),
            w_conv_out[l].astype(BF16), w_fourier_out[l].astype(BF16), w_o[l].astype(BF16),
            norm2_g[l].reshape(1, D), rw2, rwh, rb, utri)
        dest, block_expert, n_valid, n_used = _route_plan(meta_t, counts, bm, n_blocks)
        xrows = _dispatch(tq_dispatch, dest, h2r, n_rows)
        os_rows = _expert_ffn(bm, l, block_expert, n_valid, n_used, xrows,
                              w_gate_up, b_gate_up.reshape(depth, E, 1, 2 * DFF),
                              w_down, b_down.reshape(depth, E, 1, D))
        xs = tuple(_combine(geom, tq_combine, dest, os_rows, wts, x1, mod3, final_g.reshape(1, D),
                            final=(l == depth - 1)))
    return xs


def kernel(x_prompt, x_sample, c_prompt, c_sample, mod_w, mod_b, norm1_g, norm2_g, w_in, conv_w,
           conv_b, w_conv_out, w_fourier_out, w_o, router_w, router_b, w_gate_up, b_gate_up,
           w_down, b_down, final_g):
    bp, sp, _ = x_prompt.shape
    bs, ss, _ = x_sample.shape
    geom = _Geom(bp, sp, bs, ss)
    c_all = jnp.concatenate([c_prompt, c_sample], axis=0)
    y_p, y_s = _trunk(geom, x_prompt.reshape(bp * sp, D), x_sample.reshape(bs * ss, D), c_all,
                      mod_w, mod_b, norm1_g, norm2_g, w_in, conv_w, conv_b, w_conv_out,
                      w_fourier_out, w_o, router_w, router_b, w_gate_up, b_gate_up, w_down, b_down,
                      final_g)
    return (y_p.reshape(bp, sp, D), y_s.reshape(bs, ss, D))
```
